```python
import jax, jax.numpy as jnp
from jax import lax
import numpy as np

D_MODEL = 1024
BATCH = 16
SEQ = 2048
DEPTH = 4

N_MIXERS = 3
PL_DIM = 256
EPS = 1e-6
BLOCK = 128

SB_HEADS = 16
SB_HEAD_DIM = D_MODEL // SB_HEADS
RET_HEADS = 4
RET_QK_DIM = D_MODEL // RET_HEADS
RET_V_DIM = 2 * D_MODEL // RET_HEADS
ROPE_BASE = 10000.0
SGU_WIDTH = 2 * D_MODEL
SGU_GROUPS = 8
SGU_GROUP_DIM = SGU_WIDTH // SGU_GROUPS
FFN_DIM = 2816
CONV_WIDTH = 3

N_SB = len(range(0, DEPTH, N_MIXERS))
N_RET = len(range(1, DEPTH, N_MIXERS))
N_SGU = len(range(2, DEPTH, N_MIXERS))

kernel_name = "hybrid_stickbreak_retention_sgu_trunk"


def rmsnorm(x, g):
    xf = x.astype(jnp.float32)
    y = xf * lax.rsqrt(jnp.mean(xf * xf, axis=-1, keepdims=True) + EPS)
    return (y * g.astype(jnp.float32)).astype(x.dtype)


def stick_breaking_attention(h, w_in, w_out):
    b, s, _ = h.shape
    qkv = h @ w_in
    q, k, v = jnp.split(qkv, 3, axis=-1)
    to_heads = lambda a: a.reshape(b, s, SB_HEADS, SB_HEAD_DIM).transpose(0, 2, 1, 3)
    q, k, v = to_heads(q), to_heads(k), to_heads(v)
    scale = SB_HEAD_DIM ** -0.5
    outs = []
    for blk in range(s // BLOCK):
        t0 = blk * BLOCK
        end = t0 + BLOCK
        qb = q[:, :, t0:end]
        kb = k[:, :, :end]
        vb = v[:, :, :end]
        z = jnp.einsum("bhtd,bhsd->bhts", qb, kb).astype(jnp.float32) * scale
        t_idx = t0 + jnp.arange(BLOCK)[:, None]
        s_idx = jnp.arange(end)[None, :]
        mask = s_idx < t_idx
        log_keep = jnp.where(mask, jax.nn.log_sigmoid(-z), 0.0)
        tail = lax.cumsum(log_keep, axis=3, reverse=True)
        between = jnp.concatenate([tail[..., 1:], jnp.zeros_like(tail[..., :1])], axis=-1)
        a = jnp.where(mask, jnp.exp(jax.nn.log_sigmoid(z) + between), 0.0)
        outs.append(jnp.einsum("bhts,bhsd->bhtd", a.astype(vb.dtype), vb))
    o = jnp.concatenate(outs, axis=2).transpose(0, 2, 1, 3).reshape(b, s, D_MODEL)
    return o @ w_out


def rotary(x, pos):
    half = x.shape[-1] // 2
    inv_freq = 1.0 / (ROPE_BASE ** (jnp.arange(half, dtype=jnp.float32) / half))
    ang = pos[:, None] * inv_freq[None, :]
    cos = jnp.cos(ang).astype(x.dtype)
    sin = jnp.sin(ang).astype(x.dtype)
    x1, x2 = x[..., :half], x[..., half:]
    return jnp.concatenate([x1 * cos - x2 * sin, x2 * cos + x1 * sin], axis=-1)


def retention(h, w_in, w_out):
    b, s, _ = h.shape
    nc = s // BLOCK
    proj = h @ w_in
    q = proj[..., :D_MODEL].reshape(b, s, RET_HEADS, RET_QK_DIM).transpose(0, 2, 1, 3)
    k = proj[..., D_MODEL:2 * D_MODEL].reshape(b, s, RET_HEADS, RET_QK_DIM).transpose(0, 2, 1, 3)
    v = proj[..., 2 * D_MODEL:4 * D_MODEL].reshape(b, s, RET_HEADS, RET_V_DIM).transpose(0, 2, 1, 3)
    g = proj[..., 4 * D_MODEL:]
    pos = jnp.arange(s, dtype=jnp.float32)
    q = rotary(q, pos)
    k = rotary(k, pos) * (RET_QK_DIM ** -0.5)
    q = q.reshape(b, RET_HEADS, nc, BLOCK, RET_QK_DIM)
    k = k.reshape(b, RET_HEADS, nc, BLOCK, RET_QK_DIM)
    v = v.reshape(b, RET_HEADS, nc, BLOCK, RET_V_DIM)

    log_gamma = jnp.log1p(-jnp.exp2(-5.0 - jnp.arange(RET_HEADS, dtype=jnp.float32)))
    idx = jnp.arange(BLOCK, dtype=jnp.float32)
    rel = idx[:, None] - idx[None, :]
    decay_intra = jnp.where(rel >= 0, jnp.exp(log_gamma[:, None, None] * jnp.maximum(rel, 0.0)), 0.0)
    decay_q = jnp.exp(log_gamma[:, None] * (idx + 1.0))
    decay_k = jnp.exp(log_gamma[:, None] * (BLOCK - 1.0 - idx))
    decay_chunk = jnp.exp(log_gamma * BLOCK)

    scores = jnp.einsum("bhncd,bhnmd->bhncm", q, k).astype(jnp.float32) * decay_intra[None, :, None]
    intra = jnp.einsum("bhncm,bhnme->bhnce", scores.astype(v.dtype), v)

    k_dec = k * decay_k[None, :, None, :, None].astype(k.dtype)

    def step(state, xs):
        qc, kc, vc = xs
        inter = jnp.einsum("bhcd,bhde->bhce", qc, state)
        state = state * decay_chunk[None, :, None, None] + jnp.einsum("bhcd,bhce->bhde", kc, vc).astype(jnp.float32)
        return state, inter.astype(jnp.float32)

    state0 = jnp.zeros((b, RET_HEADS, RET_QK_DIM, RET_V_DIM), jnp.float32)
    _, inter = lax.scan(step, state0, (jnp.moveaxis(q, 2, 0), jnp.moveaxis(k_dec, 2, 0), jnp.moveaxis(v, 2, 0)))
    inter = jnp.moveaxis(inter, 0, 2) * decay_q[None, :, None, :, None]
    o = intra.astype(jnp.float32) + inter

    mu = jnp.mean(o, axis=-1, keepdims=True)
    var = jnp.mean(jnp.square(o - mu), axis=-1, keepdims=True)
    o = (o - mu) * lax.rsqrt(var + EPS)
    o = o.transpose(0, 2, 3, 1, 4).reshape(b, s, RET_HEADS * RET_V_DIM)
    y = (jax.nn.silu(g.astype(jnp.float32)) * o).astype(h.dtype)
    return y @ w_out


def chunked_sgu(h, w_in, sgu_norm, w_s, b_s, w_out):
    b, s, _ = h.shape
    nc = s // BLOCK
    z = jax.nn.gelu(h @ w_in)
    u, v = jnp.split(z, 2, axis=-1)
    v = rmsnorm(v, sgu_norm).reshape(b, nc, BLOCK, SGU_GROUPS, SGU_GROUP_DIM)
    causal = jnp.tril(jnp.ones((BLOCK, BLOCK), dtype=w_s.dtype))
    w = w_s * causal[None]
    mixed = jnp.einsum("gts,bnsgc->bntgc", w, v) + b_s.T[None, None, :, :, None]
    return (u * mixed.reshape(b, s, SGU_WIDTH)) @ w_out


def conv_gated_ffn(h, w_in, conv_w, conv_b, w_out):
    s = h.shape[1]
    a = h @ w_in
    gate, up = a[..., :FFN_DIM], a[..., FFN_DIM:]
    gpad = jnp.pad(gate, ((0, 0), (CONV_WIDTH - 1, 0), (0, 0)))
    conv = conv_b + sum(gpad[:, tap:tap + s] * conv_w[tap] for tap in range(CONV_WIDTH))
    return (jax.nn.silu(conv) * up) @ w_out


def per_layer_embed(x, p_i, norm_g, w_gate, w_proj):
    gate = jax.nn.sigmoid((rmsnorm(x, norm_g) @ w_gate).astype(jnp.float32)).astype(x.dtype)
    return x + gate * (p_i @ w_proj)


def setup_inputs(seed: int = 0) -> dict:
    key = jax.random.key(seed)
    ks = iter(jax.random.split(key, 32))
    nrm = lambda shape, fan: jax.random.normal(next(ks), shape, jnp.float32) * (fan ** -0.5)
    gain = lambda shape: 1.0 + 0.02 * jax.random.normal(next(ks), shape, jnp.float32)
    return {
        "x": jax.random.normal(next(ks), (BATCH, SEQ, D_MODEL), jnp.float32),
        "p": jax.random.normal(next(ks), (DEPTH, BATCH, SEQ, PL_DIM), jnp.float32),
        "norm_mix": gain((DEPTH, D_MODEL)),
        "norm_ffn": gain((DEPTH, D_MODEL)),
        "norm_pl": gain((DEPTH, D_MODEL)),
        "norm_final": gain((D_MODEL,)),
        "sb_w_in": nrm((N_SB, D_MODEL, 3 * D_MODEL), D_MODEL),
        "sb_w_out": nrm((N_SB, D_MODEL, D_MODEL), D_MODEL),
        "ret_w_in": nrm((N_RET, D_MODEL, 6 * D_MODEL), D_MODEL),
        "ret_w_out": nrm((N_RET, 2 * D_MODEL, D_MODEL), 2 * D_MODEL),
        "sgu_w_in": nrm((N_SGU, D_MODEL, 2 * SGU_WIDTH), D_MODEL),
        "sgu_norm": gain((N_SGU, SGU_WIDTH)),
        "sgu_w_s": nrm((N_SGU, SGU_GROUPS, BLOCK, BLOCK), BLOCK),
        "sgu_b_s": gain((N_SGU, SGU_GROUPS, BLOCK)),
        "sgu_w_out": nrm((N_SGU, SGU_WIDTH, D_MODEL), SGU_WIDTH),
        "ffn_w_in": nrm((DEPTH, D_MODEL, 2 * FFN_DIM), D_MODEL),
        "ffn_conv_w": nrm((DEPTH, CONV_WIDTH, FFN_DIM), CONV_WIDTH),
        "ffn_conv_b": 0.02 * jax.random.normal(next(ks), (DEPTH, FFN_DIM), jnp.float32),
        "ffn_w_out": nrm((DEPTH, FFN_DIM, D_MODEL), FFN_DIM),
        "pl_w_gate": nrm((DEPTH, D_MODEL, D_MODEL), D_MODEL),
        "pl_w_proj": nrm((DEPTH, PL_DIM, D_MODEL), PL_DIM),
    }


def reference(x, p, norm_mix, norm_ffn, norm_pl, norm_final, sb_w_in, sb_w_out, ret_w_in, ret_w_out,
              sgu_w_in, sgu_norm, sgu_w_s, sgu_b_s, sgu_w_out, ffn_w_in, ffn_conv_w, ffn_conv_b, ffn_w_out,
              pl_w_gate, pl_w_proj):
    for i in range(DEPTH):
        kind = i % N_MIXERS
        j = i // N_MIXERS
        h = rmsnorm(x, norm_mix[i])
        if kind == 0:
            m = stick_breaking_attention(h, sb_w_in[j], sb_w_out[j])
        elif kind == 1:
            m = retention(h, ret_w_in[j], ret_w_out[j])
        else:
            m = chunked_sgu(h, sgu_w_in[j], sgu_norm[j], sgu_w_s[j], sgu_b_s[j], sgu_w_out[j])
        x = x + m
        x = x + conv_gated_ffn(rmsnorm(x, norm_ffn[i]), ffn_w_in[i], ffn_conv_w[i], ffn_conv_b[i], ffn_w_out[i])
        x = per_layer_embed(x, p[i], norm_pl[i], pl_w_gate[i], pl_w_proj[i])
    return rmsnorm(x, norm_final)
```

```python
import functools

import jax
import jax.numpy as jnp
from jax import lax
from jax.experimental import pallas as pl
from jax.experimental.pallas import tpu as pltpu

D_MODEL = 1024
DEPTH = 4
N_MIXERS = 3
PL_DIM = 256
EPS = 1e-6
BLOCK = 128

SB_HEADS = 16
SB_HEAD_DIM = D_MODEL // SB_HEADS
RET_HEADS = 4
RET_QK_DIM = D_MODEL // RET_HEADS
RET_V_DIM = 2 * D_MODEL // RET_HEADS
ROPE_BASE = 10000.0
SGU_WIDTH = 2 * D_MODEL
SGU_GROUPS = 8
SGU_GROUP_DIM = SGU_WIDTH // SGU_GROUPS
FFN_DIM = 2816
CONV_WIDTH = 3

LANES = 128
SUBLANES = 8
MXU_DIM = 256
VMEM_LIMIT_BYTES = 56 * 1024 * 1024

ROW_TILE = 1024
SB_KEY_BLOCK = 2 * BLOCK

F32 = jnp.float32
BF16 = jnp.bfloat16


def _params(*semantics):
    return pltpu.CompilerParams(dimension_semantics=semantics, vmem_limit_bytes=VMEM_LIMIT_BYTES)


def _rmsnorm(x, g):
    return x * lax.rsqrt(jnp.mean(x * x, axis=-1, keepdims=True) + EPS) * g


def _dot(a, b):
    return jnp.dot(a, b, preferred_element_type=F32)


def _dot_nt(a, b):
    return lax.dot_general(a, b, (((1,), (1,)), ((), ())), preferred_element_type=F32)


def _norm_matmul_kernel(x_ref, g_ref, w_ref, o_ref, h_ref, *, gelu):
    @pl.when(pl.program_id(1) == 0)
    def _():
        h_ref[...] = _rmsnorm(x_ref[...], g_ref[...]).astype(BF16)

    acc = _dot(h_ref[...], w_ref[...])
    if gelu:
        acc = jax.nn.gelu(acc)
    o_ref[...] = acc.astype(o_ref.dtype)


def _norm_matmul(x, g, w, *, gelu=False, col_tile=1024):
    rows, d = x.shape
    n = w.shape[1]
    return pl.pallas_call(
        functools.partial(_norm_matmul_kernel, gelu=gelu),
        grid=(rows // ROW_TILE, n // col_tile),
        in_specs=[
            pl.BlockSpec((ROW_TILE, d), lambda i, j: (i, 0)),
            pl.BlockSpec((1, d), lambda i, j: (0, 0)),
            pl.BlockSpec((d, col_tile), lambda i, j: (0, j)),
        ],
        out_specs=pl.BlockSpec((ROW_TILE, col_tile), lambda i, j: (i, j)),
        out_shape=jax.ShapeDtypeStruct((rows, n), BF16),
        scratch_shapes=[pltpu.VMEM((ROW_TILE, d), BF16)],
        compiler_params=_params("parallel", "arbitrary"),
        name="norm_matmul",
    )(x, g.reshape(1, d), w)


def _matmul_residual_kernel(a_ref, w_ref, x_ref, o_ref):
    o_ref[...] = x_ref[...] + _dot(a_ref[...], w_ref[...])


def _matmul_residual(a, w, x):
    rows, k = a.shape
    d = w.shape[1]
    return pl.pallas_call(
        _matmul_residual_kernel,
        grid=(rows // ROW_TILE,),
        in_specs=[
            pl.BlockSpec((ROW_TILE, k), lambda i: (i, 0)),
            pl.BlockSpec((k, d), lambda i: (0, 0)),
            pl.BlockSpec((ROW_TILE, d), lambda i: (i, 0)),
        ],
        out_specs=pl.BlockSpec((ROW_TILE, d), lambda i: (i, 0)),
        out_shape=jax.ShapeDtypeStruct((rows, d), F32),
        compiler_params=_params("parallel"),
        name="matmul_residual",
    )(a, w, x)


def _sb_scores(qm, kb, tri, carry, mask):
    z = _dot_nt(qm, kb)
    sp = jnp.log(1.0 + jnp.exp(-jnp.abs(z))) - jnp.minimum(z, 0.0)
    nlk = sp + z
    if mask is not None:
        nlk = jnp.where(mask, nlk, 0.0)
    hi = nlk.astype(BF16)
    lo = (nlk - hi.astype(F32)).astype(BF16)
    neg_suffix = _dot(hi, tri) + _dot(lo, tri)
    a = jnp.exp(neg_suffix - carry - sp)
    if mask is not None:
        a = jnp.where(mask, a, 0.0)
    carry = carry + jnp.sum(nlk, axis=-1, keepdims=True)
    return a.astype(BF16), carry


def _sb_attention_kernel(q_ref, k_ref, v_ref, o_ref):
    seq = q_ref.shape[0]
    kblk = SB_KEY_BLOCK
    lane = lax.broadcasted_iota(jnp.int32, (BLOCK, LANES), 1)
    head_lanes = [lane < SB_HEAD_DIM, lane >= SB_HEAD_DIM]
    row = lax.broadcasted_iota(jnp.int32, (kblk, kblk), 0)
    col = lax.broadcasted_iota(jnp.int32, (kblk, kblk), 1)
    tri = jnp.where(row > col, -1.0, 0.0).astype(BF16)
    q_row = lax.broadcasted_iota(jnp.int32, (BLOCK, kblk), 0)
    k_col = lax.broadcasted_iota(jnp.int32, (BLOCK, kblk), 1)
    scale = SB_HEAD_DIM ** -0.5

    def q_block(i, _):
        t0 = pl.multiple_of(i * BLOCK, BLOCK)
        q = q_ref[pl.ds(t0, BLOCK), :] * jnp.asarray(scale, BF16)
        qms = [jnp.where(m, q, jnp.zeros_like(q)) for m in head_lanes]
        last = (i * BLOCK) // kblk
        s0 = pl.multiple_of(last * kblk, kblk)
        diag_mask = (k_col + s0) < (q_row + t0)
        kb = k_ref[pl.ds(s0, kblk), :]
        vb = v_ref[pl.ds(s0, kblk), :]
        state = []
        for qm in qms:
            a, carry = _sb_scores(qm, kb, tri, jnp.zeros((BLOCK, 1), F32), diag_mask)
            state += [_dot(a, vb), carry]

        def key_block(jj, state):
            s0 = pl.multiple_of((last - 1 - jj) * kblk, kblk)
            kb = k_ref[pl.ds(s0, kblk), :]
            vb = v_ref[pl.ds(s0, kblk), :]
            out = []
            for h, qm in enumerate(qms):
                a, carry = _sb_scores(qm, kb, tri, state[2 * h + 1], None)
                out += [state[2 * h] + _dot(a, vb), carry]
            return tuple(out)

        state = lax.fori_loop(0, last, key_block, tuple(state))
        o_ref[pl.ds(t0, BLOCK), :] = jnp.where(head_lanes[0], state[0], state[2]).astype(o_ref.dtype)
        return 0

    lax.fori_loop(0, seq // BLOCK, q_block, 0)


def _sb_attention(qkv):
    b, s, _ = qkv.shape
    col_blocks = D_MODEL // LANES
    return pl.pallas_call(
        _sb_attention_kernel,
        grid=(b, col_blocks),
        in_specs=[
            pl.BlockSpec((None, s, LANES), lambda bi, c: (bi, 0, c)),
            pl.BlockSpec((None, s, LANES), lambda bi, c: (bi, 0, col_blocks + c)),
            pl.BlockSpec((None, s, LANES), lambda bi, c: (bi, 0, 2 * col_blocks + c)),
        ],
        out_specs=pl.BlockSpec((None, s, LANES), lambda bi, c: (bi, 0, c)),
        out_shape=jax.ShapeDtypeStruct((b, s, D_MODEL), BF16),
        compiler_params=_params("parallel", "parallel"),
        name="sb_attention",
    )(qkv, qkv, qkv)


def _rotate(x, cos, sin):
    half = x.shape[-1] // 2
    x1, x2 = x[:, :half], x[:, half:]
    return jnp.concatenate([x1 * cos - x2 * sin, x2 * cos + x1 * sin], axis=-1)


def _retention_kernel(lg_ref, q_ref, k_ref, v_ref, g_ref, cos_ref, sin_ref, o_ref, state_ref):
    seq = q_ref.shape[0]
    lg = lg_ref[pl.program_id(1)]
    r = lax.broadcasted_iota(jnp.int32, (BLOCK, BLOCK), 0)
    c = lax.broadcasted_iota(jnp.int32, (BLOCK, BLOCK), 1)
    rel = (r - c).astype(F32)
    decay_intra = jnp.where(rel >= 0, jnp.exp(lg * jnp.maximum(rel, 0.0)), 0.0)
    pos_k = lax.broadcasted_iota(jnp.int32, (BLOCK, RET_QK_DIM), 0).astype(F32)
    decay_k = jnp.exp(lg * (BLOCK - 1.0 - pos_k))
    pos_v = lax.broadcasted_iota(jnp.int32, (BLOCK, RET_V_DIM), 0).astype(F32)
    decay_q = jnp.exp(lg * (pos_v + 1.0))
    decay_chunk = jnp.exp(jnp.full((1, RET_V_DIM), lg * BLOCK, F32))
    state_ref[...] = jnp.zeros_like(state_ref)

    def chunk(n, _):
        r0 = pl.multiple_of(n * BLOCK, BLOCK)
        rows = pl.ds(r0, BLOCK)
        cos = cos_ref[rows, :]
        sin = sin_ref[rows, :]
        q = _rotate(q_ref[rows, :].astype(F32), cos, sin)
        k = _rotate(k_ref[rows, :].astype(F32), cos, sin) * (RET_QK_DIM ** -0.5)
        v = v_ref[rows, :]
        qb = q.astype(BF16)
        scores = _dot_nt(qb, k.astype(BF16)) * decay_intra
        intra = _dot(scores.astype(BF16), v)
        state = state_ref[...]
        inter = _dot(qb, state.astype(BF16))
        k_dec_t = (k * decay_k).T.astype(BF16)
        state_ref[...] = state * decay_chunk + _dot(k_dec_t, v)
        o = intra + inter * decay_q
        mu = jnp.mean(o, axis=-1, keepdims=True)
        oc = o - mu
        var = jnp.mean(oc * oc, axis=-1, keepdims=True)
        on = oc * lax.rsqrt(var + EPS)
        gate = g_ref[rows, :].astype(F32)
        o_ref[rows, :] = (gate * jax.nn.sigmoid(gate) * on).astype(o_ref.dtype)
        return 0

    lax.fori_loop(0, seq // BLOCK, chunk, 0)


def _retention(proj, log_gamma, cos, sin):
    b, s, _ = proj.shape
    qk_blocks = D_MODEL // RET_QK_DIM
    v_blocks = 2 * D_MODEL // RET_V_DIM
    half = RET_QK_DIM // 2
    return pl.pallas_call(
        _retention_kernel,
        grid=(b, RET_HEADS),
        in_specs=[
            pl.BlockSpec(memory_space=pltpu.SMEM),
            pl.BlockSpec((None, s, RET_QK_DIM), lambda bi, h: (bi, 0, h)),
            pl.BlockSpec((None, s, RET_QK_DIM), lambda bi, h: (bi, 0, qk_blocks + h)),
            pl.BlockSpec((None, s, RET_V_DIM), lambda bi, h: (bi, 0, v_blocks + h)),
            pl.BlockSpec((None, s, RET_V_DIM), lambda bi, h: (bi, 0, 2 * v_blocks + h)),
            pl.BlockSpec((s, half), lambda bi, h: (0, 0)),
            pl.BlockSpec((s, half), lambda bi, h: (0, 0)),
        ],
        out_specs=pl.BlockSpec((None, s, RET_V_DIM), lambda bi, h: (bi, 0, h)),
        out_shape=jax.ShapeDtypeStruct((b, s, RET_HEADS * RET_V_DIM), BF16),
        scratch_shapes=[pltpu.VMEM((RET_QK_DIM, RET_V_DIM), F32)],
        compiler_params=_params("parallel", "parallel"),
        name="retention",
    )(log_gamma, proj, proj, proj, proj, cos, sin)


def _sgu_kernel(z_ref, x_ref, nrm_ref, ws_ref, bs_ref, wo_ref, o_ref, vn_ref, um_ref):
    rows = z_ref.shape[0]
    v = z_ref[:, SGU_WIDTH:].astype(F32)
    vn_ref[...] = _rmsnorm(v, nrm_ref[...]).astype(BF16)
    r = lax.broadcasted_iota(jnp.int32, (BLOCK, BLOCK), 0)
    c = lax.broadcasted_iota(jnp.int32, (BLOCK, BLOCK), 1)
    causal = c <= r
    for g in range(SGU_GROUPS):
        w = jnp.where(causal, ws_ref[g], 0.0).astype(BF16)
        bias = bs_ref[g]
        cols = slice(g * SGU_GROUP_DIM, (g + 1) * SGU_GROUP_DIM)
        for n in range(rows // BLOCK):
            rs = slice(n * BLOCK, (n + 1) * BLOCK)
            mixed = _dot(w, vn_ref[rs, cols]) + bias
            um_ref[rs, cols] = (z_ref[rs, cols].astype(F32) * mixed).astype(BF16)
    o_ref[...] = x_ref[...] + _dot(um_ref[...], wo_ref[...])


def _sgu(z, x, sgu_norm, w_s, b_s, w_out, *, row_tile=512):
    rows, d = x.shape
    return pl.pallas_call(
        _sgu_kernel,
        grid=(rows // row_tile,),
        in_specs=[
            pl.BlockSpec((row_tile, 2 * SGU_WIDTH), lambda i: (i, 0)),
            pl.BlockSpec((row_tile, d), lambda i: (i, 0)),
            pl.BlockSpec((1, SGU_WIDTH), lambda i: (0, 0)),
            pl.BlockSpec((SGU_GROUPS, BLOCK, BLOCK), lambda i: (0, 0, 0)),
            pl.BlockSpec((SGU_GROUPS, BLOCK, 1), lambda i: (0, 0, 0)),
            pl.BlockSpec((SGU_WIDTH, d), lambda i: (0, 0)),
        ],
        out_specs=pl.BlockSpec((row_tile, d), lambda i: (i, 0)),
        out_shape=jax.ShapeDtypeStruct((rows, d), F32),
        scratch_shapes=[pltpu.VMEM((row_tile, SGU_WIDTH), BF16), pltpu.VMEM((row_tile, SGU_WIDTH), BF16)],
        compiler_params=_params("parallel"),
        name="sgu",
    )(z, x, sgu_norm.reshape(1, SGU_WIDTH), w_s, b_s.reshape(SGU_GROUPS, BLOCK, 1), w_out)


def _ffn_kernel(x_ref, g_ref, wg_ref, wu_ref, cw_ref, cb_ref, wo_ref, o_ref,
                h_ref, acc_ref, carry_ref, shift_ref, *, tiles_per_seq):
    i = pl.program_id(0)
    j = pl.program_id(1)
    rows = x_ref.shape[0]
    halo = SUBLANES

    @pl.when(j == 0)
    def _():
        h_ref[...] = _rmsnorm(x_ref[...], g_ref[...]).astype(BF16)
        acc_ref[...] = jnp.zeros_like(acc_ref)

    h = h_ref[...]
    gate = _dot(h, wg_ref[...])
    up = _dot(h, wu_ref[...])
    seq_start = (i % tiles_per_seq) == 0
    shift_ref[0:halo, :] = jnp.where(seq_start, 0.0, carry_ref[j])
    shift_ref[halo:, :] = gate
    carry_ref[j] = gate[rows - halo:, :]
    g1 = shift_ref[halo - 1:halo - 1 + rows, :]
    g2 = shift_ref[halo - 2:halo - 2 + rows, :]
    conv = cb_ref[...] + cw_ref[0:1, :] * g2 + cw_ref[1:2, :] * g1 + cw_ref[2:3, :] * gate
    act = (conv * jax.nn.sigmoid(conv) * up).astype(BF16)
    acc_ref[...] += _dot(act, wo_ref[...])

    @pl.when(j == pl.num_programs(1) - 1)
    def _():
        o_ref[...] = x_ref[...] + acc_ref[...]


def _ffn(x, g, w_in, conv_w, conv_b, w_out, *, seq, f_tile=256):
    rows, d = x.shape
    nf = FFN_DIM // f_tile
    return pl.pallas_call(
        functools.partial(_ffn_kernel, tiles_per_seq=seq // ROW_TILE),
        grid=(rows // ROW_TILE, nf),
        in_specs=[
            pl.BlockSpec((ROW_TILE, d), lambda i, j: (i, 0)),
            pl.BlockSpec((1, d), lambda i, j: (0, 0)),
            pl.BlockSpec((d, f_tile), lambda i, j: (0, j)),
            pl.BlockSpec((d, f_tile), lambda i, j: (0, nf + j)),
            pl.BlockSpec((CONV_WIDTH, f_tile), lambda i, j: (0, j)),
            pl.BlockSpec((1, f_tile), lambda i, j: (0, j)),
            pl.BlockSpec((f_tile, d), lambda i, j: (j, 0)),
        ],
        out_specs=pl.BlockSpec((ROW_TILE, d), lambda i, j: (i, 0)),
        out_shape=jax.ShapeDtypeStruct((rows, d), F32),
        scratch_shapes=[
            pltpu.VMEM((ROW_TILE, d), BF16),
            pltpu.VMEM((ROW_TILE, d), F32),
            pltpu.VMEM((nf, SUBLANES, f_tile), F32),
            pltpu.VMEM((SUBLANES + ROW_TILE, f_tile), F32),
        ],
        compiler_params=_params("arbitrary", "arbitrary"),
        name="conv_ffn",
    )(x, g.reshape(1, d), w_in, w_in, conv_w, conv_b.reshape(1, FFN_DIM), w_out)


def _ple_kernel(x_ref, p_ref, g_ref, wg_ref, wp_ref, gf_ref, o_ref, *, final):
    x = x_ref[...]
    h = _rmsnorm(x, g_ref[...]).astype(BF16)
    gate = jax.nn.sigmoid(_dot(h, wg_ref[...]))
    y = x + gate * _dot(p_ref[...].astype(BF16), wp_ref[...])
    if final:
        y = _rmsnorm(y, gf_ref[...])
    o_ref[...] = y


def _ple(x, p, layer, g, w_gate, w_proj, g_final, *, final):
    rows, d = x.shape
    return pl.pallas_call(
        functools.partial(_ple_kernel, final=final),
        grid=(rows // ROW_TILE,),
        in_specs=[
            pl.BlockSpec((ROW_TILE, d), lambda i: (i, 0)),
            pl.BlockSpec((None, ROW_TILE, PL_DIM), lambda i: (layer, i, 0)),
            pl.BlockSpec((1, d), lambda i: (0, 0)),
            pl.BlockSpec((d, d), lambda i: (0, 0)),
            pl.BlockSpec((PL_DIM, d), lambda i: (0, 0)),
            pl.BlockSpec((1, d), lambda i: (0, 0)),
        ],
        out_specs=pl.BlockSpec((ROW_TILE, d), lambda i: (i, 0)),
        out_shape=jax.ShapeDtypeStruct((rows, d), F32),
        compiler_params=_params("parallel"),
        name="per_layer_embed",
    )(x, p, g.reshape(1, d), w_gate, w_proj, g_final.reshape(1, d))


def _rotary_tables(seq):
    half = RET_QK_DIM // 2
    inv_freq = 1.0 / (ROPE_BASE ** (jnp.arange(half, dtype=F32) / half))
    ang = jnp.arange(seq, dtype=F32)[:, None] * inv_freq[None, :]
    return jnp.cos(ang), jnp.sin(ang)


def kernel(x, p, norm_mix, norm_ffn, norm_pl, norm_final, sb_w_in, sb_w_out, ret_w_in, ret_w_out, sgu_w_in, sgu_norm, sgu_w_s, sgu_b_s, sgu_w_out, ffn_w_in, ffn_conv_w, ffn_conv_b, ffn_w_out, pl_w_gate, pl_w_proj):
    b, s, d = x.shape
    rows = b * s
    xf = x.reshape(rows, d)
    pf = p.reshape(DEPTH, rows, PL_DIM)
    log_gamma = jnp.log1p(-jnp.exp2(-5.0 - jnp.arange(RET_HEADS, dtype=F32)))
    cos, sin = _rotary_tables(s)
    for i in range(DEPTH):
        kind = i % N_MIXERS
        j = i // N_MIXERS
        if kind == 0:
            qkv = _norm_matmul(xf, norm_mix[i], sb_w_in[j].astype(BF16))
            o = _sb_attention(qkv.reshape(b, s, 3 * d))
            xf = _matmul_residual(o.reshape(rows, d), sb_w_out[j].astype(BF16), xf)
        elif kind == 1:
            proj = _norm_matmul(xf, norm_mix[i], ret_w_in[j].astype(BF16))
            y = _retention(proj.reshape(b, s, 6 * d), log_gamma, cos, sin)
            xf = _matmul_residual(y.reshape(rows, 2 * d), ret_w_out[j].astype(BF16), xf)
        else:
            z = _norm_matmul(xf, norm_mix[i], sgu_w_in[j].astype(BF16), gelu=True)
            xf = _sgu(z, xf, sgu_norm[j], sgu_w_s[j], sgu_b_s[j], sgu_w_out[j].astype(BF16))
        xf = _ffn(xf, norm_ffn[i], ffn_w_in[i].astype(BF16), ffn_conv_w[i], ffn_conv_b[i],
                  ffn_w_out[i].astype(BF16), seq=s)
        xf = _ple(xf, pf, i, norm_pl[i], pl_w_gate[i].astype(BF16), pl_w_proj[i].astype(BF16),
                  norm_final, final=(i == DEPTH - 1))
    return xf.reshape(b, s, d)
```

```python
import functools

import jax
import jax.numpy as jnp
from jax import lax
from jax.experimental import pallas as pl
from jax.experimental.pallas import tpu as pltpu

D_MODEL = 1024
DEPTH = 4
N_MIXERS = 3
PL_DIM = 256
EPS = 1e-6
BLOCK = 128

SB_HEADS = 16
SB_HEAD_DIM = D_MODEL // SB_HEADS
RET_HEADS = 4
RET_QK_DIM = D_MODEL // RET_HEADS
RET_V_DIM = 2 * D_MODEL // RET_HEADS
ROPE_BASE = 10000.0
SGU_WIDTH = 2 * D_MODEL
SGU_GROUPS = 8
SGU_GROUP_DIM = SGU_WIDTH // SGU_GROUPS
FFN_DIM = 2816
CONV_WIDTH = 3

LANES = 128
SUBLANES = 8
MXU_DIM = 256
VMEM_LIMIT_BYTES = 56 * 1024 * 1024

ROW_TILE = 1024
SB_BLOCK = MXU_DIM
LOG2E = 1.4426950408889634

F32 = jnp.float32
BF16 = jnp.bfloat16


def _params(*semantics):
    return pltpu.CompilerParams(dimension_semantics=semantics, vmem_limit_bytes=VMEM_LIMIT_BYTES)


def _rmsnorm(x, g):
    return x * lax.rsqrt(jnp.mean(x * x, axis=-1, keepdims=True) + EPS) * g


def _dot(a, b):
    return jnp.dot(a, b, preferred_element_type=F32)


def _dot_nt(a, b):
    return lax.dot_general(a, b, (((1,), (1,)), ((), ())), preferred_element_type=F32)


def _norm_matmul_kernel(x_ref, g_ref, w_ref, o_ref, h_ref, *, gelu):
    @pl.when(pl.program_id(1) == 0)
    def _():
        h_ref[...] = _rmsnorm(x_ref[...], g_ref[...]).astype(BF16)

    acc = _dot(h_ref[...], w_ref[...])
    if gelu:
        acc = jax.nn.gelu(acc)
    o_ref[...] = acc.astype(o_ref.dtype)


def _norm_matmul(x, g, w, *, gelu=False, col_tile=1024):
    rows, d = x.shape
    n = w.shape[1]
    return pl.pallas_call(
        functools.partial(_norm_matmul_kernel, gelu=gelu),
        grid=(rows // ROW_TILE, n // col_tile),
        in_specs=[
            pl.BlockSpec((ROW_TILE, d), lambda i, j: (i, 0)),
            pl.BlockSpec((1, d), lambda i, j: (0, 0)),
            pl.BlockSpec((d, col_tile), lambda i, j: (0, j)),
        ],
        out_specs=pl.BlockSpec((ROW_TILE, col_tile), lambda i, j: (i, j)),
        out_shape=jax.ShapeDtypeStruct((rows, n), BF16),
        scratch_shapes=[pltpu.VMEM((ROW_TILE, d), BF16)],
        compiler_params=_params("parallel", "arbitrary"),
        name="norm_matmul",
    )(x, g.reshape(1, d), w)


def _matmul_residual_kernel(a_ref, w_ref, x_ref, o_ref):
    o_ref[...] = x_ref[...] + _dot(a_ref[...], w_ref[...])


def _matmul_residual(a, w, x):
    rows, k = a.shape
    d = w.shape[1]
    return pl.pallas_call(
        _matmul_residual_kernel,
        grid=(rows // ROW_TILE,),
        in_specs=[
            pl.BlockSpec((ROW_TILE, k), lambda i: (i, 0)),
            pl.BlockSpec((k, d), lambda i: (0, 0)),
            pl.BlockSpec((ROW_TILE, d), lambda i: (i, 0)),
        ],
        out_specs=pl.BlockSpec((ROW_TILE, d), lambda i: (i, 0)),
        out_shape=jax.ShapeDtypeStruct((rows, d), F32),
        compiler_params=_params("parallel"),
        name="matmul_residual",
    )(a, w, x)


def _sb_block(qm, kb, vb, tri2, carry, mask):
    z = _dot_nt(qm, kb)
    e = jnp.exp2(jnp.abs(z) * (-LOG2E))
    nlk = jnp.log(1.0 + e) + jnp.maximum(z, 0.0)
    if mask is not None:
        nlk = jnp.where(mask, nlk, 0.0)
    hi = nlk.astype(BF16)
    lo = (nlk - hi.astype(F32)).astype(BF16)
    neg_suffix = _dot(jnp.concatenate([hi, lo], axis=1), tri2)
    a = jnp.exp2((neg_suffix + (z - carry)) * LOG2E)
    if mask is not None:
        a = jnp.where(mask, a, 0.0)
    carry = carry + jnp.sum(nlk, axis=-1, keepdims=True)
    return _dot(a.astype(BF16), vb), carry


def _sb_attention_kernel(q_ref, k_ref, v_ref, o_ref):
    seq = q_ref.shape[0]
    blk = SB_BLOCK
    sup = 2 * blk
    first_head = lax.broadcasted_iota(jnp.int32, (blk, LANES), 1) < SB_HEAD_DIM
    r = lax.broadcasted_iota(jnp.int32, (2 * blk, blk), 0) & (blk - 1)
    c = lax.broadcasted_iota(jnp.int32, (2 * blk, blk), 1)
    tri2 = jnp.where(r >= c, -1.0, 0.0).astype(BF16)
    diag_mask = c < r
    scale = jnp.asarray(SB_HEAD_DIM ** -0.5, BF16)

    def stack_heads(q):
        zero = jnp.zeros_like(q)
        return jnp.concatenate([jnp.where(first_head, q, zero), jnp.where(first_head, zero, q)], axis=0)

    def merge_heads(acc):
        return jnp.where(first_head, acc[:blk], acc[blk:]).astype(o_ref.dtype)

    def super_block(m, _):
        r0 = pl.multiple_of(m * sup, sup)
        q_lo = stack_heads(q_ref[pl.ds(r0, blk), :] * scale)
        q_hi = stack_heads(q_ref[pl.ds(r0 + blk, blk), :] * scale)
        k_lo, v_lo = k_ref[pl.ds(r0, blk), :], v_ref[pl.ds(r0, blk), :]
        k_hi, v_hi = k_ref[pl.ds(r0 + blk, blk), :], v_ref[pl.ds(r0 + blk, blk), :]
        no_carry = jnp.zeros((2 * blk, 1), F32)
        acc_lo, car_lo = _sb_block(q_lo, k_lo, v_lo, tri2, no_carry, diag_mask)
        acc_hi, car_hi = _sb_block(q_hi, k_hi, v_hi, tri2, no_carry, diag_mask)
        part, car_hi = _sb_block(q_hi, k_lo, v_lo, tri2, car_hi, None)
        q_all = jnp.concatenate([q_lo, q_hi], axis=0)
        acc = jnp.concatenate([acc_lo, acc_hi + part], axis=0)
        car = jnp.concatenate([car_lo, car_hi], axis=0)

        def key_super(jj, state):
            acc, car = state
            s0 = pl.multiple_of((m - 1 - jj) * sup, sup)
            for half in (1, 0):
                rows = pl.ds(s0 + half * blk, blk)
                part, car = _sb_block(q_all, k_ref[rows, :], v_ref[rows, :], tri2, car, None)
                acc = acc + part
            return acc, car

        acc, car = lax.fori_loop(0, m, key_super, (acc, car))
        o_ref[pl.ds(r0, blk), :] = merge_heads(acc[:2 * blk])
        o_ref[pl.ds(r0 + blk, blk), :] = merge_heads(acc[2 * blk:])
        return 0

    lax.fori_loop(0, seq // sup, super_block, 0)


def _sb_attention(qkv):
    b, s, _ = qkv.shape
    col_blocks = D_MODEL // LANES
    return pl.pallas_call(
        _sb_attention_kernel,
        grid=(b, col_blocks),
        in_specs=[
            pl.BlockSpec((None, s, LANES), lambda bi, c: (bi, 0, c)),
            pl.BlockSpec((None, s, LANES), lambda bi, c: (bi, 0, col_blocks + c)),
            pl.BlockSpec((None, s, LANES), lambda bi, c: (bi, 0, 2 * col_blocks + c)),
        ],
        out_specs=pl.BlockSpec((None, s, LANES), lambda bi, c: (bi, 0, c)),
        out_shape=jax.ShapeDtypeStruct((b, s, D_MODEL), BF16),
        compiler_params=_params("parallel", "parallel"),
        name="sb_attention",
    )(qkv, qkv, qkv)


def _rotate(x, cos, sin):
    half = x.shape[-1] // 2
    x1, x2 = x[:, :half], x[:, half:]
    return jnp.concatenate([x1 * cos - x2 * sin, x2 * cos + x1 * sin], axis=-1)


def _retention_kernel(lg_ref, q_ref, k_ref, v_ref, g_ref, cos_ref, sin_ref, o_ref, state_ref):
    seq = q_ref.shape[0]
    lg = lg_ref[pl.program_id(1)]
    r = lax.broadcasted_iota(jnp.int32, (BLOCK, BLOCK), 0)
    c = lax.broadcasted_iota(jnp.int32, (BLOCK, BLOCK), 1)
    rel = (r - c).astype(F32)
    decay_intra = jnp.where(rel >= 0, jnp.exp(lg * jnp.maximum(rel, 0.0)), 0.0)
    pos_k = lax.broadcasted_iota(jnp.int32, (BLOCK, RET_QK_DIM), 0).astype(F32)
    decay_k = jnp.exp(lg * (BLOCK - 1.0 - pos_k))
    pos_v = lax.broadcasted_iota(jnp.int32, (BLOCK, RET_V_DIM), 0).astype(F32)
    decay_q = jnp.exp(lg * (pos_v + 1.0))
    decay_chunk = jnp.exp(jnp.full((1, RET_V_DIM), lg * BLOCK, F32))
    state_ref[...] = jnp.zeros_like(state_ref)

    def chunk(n, _):
        r0 = pl.multiple_of(n * BLOCK, BLOCK)
        rows = pl.ds(r0, BLOCK)
        cos = cos_ref[rows, :]
        sin = sin_ref[rows, :]
        q = _rotate(q_ref[rows, :].astype(F32), cos, sin)
        k = _rotate(k_ref[rows, :].astype(F32), cos, sin) * (RET_QK_DIM ** -0.5)
        v = v_ref[rows, :]
        qb = q.astype(BF16)
        scores = _dot_nt(qb, k.astype(BF16)) * decay_intra
        intra = _dot(scores.astype(BF16), v)
        state = state_ref[...]
        inter = _dot(qb, state.astype(BF16))
        k_dec_t = (k * decay_k).T.astype(BF16)
        state_ref[...] = state * decay_chunk + _dot(k_dec_t, v)
        o = intra + inter * decay_q
        mu = jnp.mean(o, axis=-1, keepdims=True)
        oc = o - mu
        var = jnp.mean(oc * oc, axis=-1, keepdims=True)
        on = oc * lax.rsqrt(var + EPS)
        gate = g_ref[rows, :].astype(F32)
        o_ref[rows, :] = (gate * jax.nn.sigmoid(gate) * on).astype(o_ref.dtype)
        return 0

    lax.fori_loop(0, seq // BLOCK, chunk, 0)


def _retention(proj, log_gamma, cos, sin):
    b, s, _ = proj.shape
    qk_blocks = D_MODEL // RET_QK_DIM
    v_blocks = 2 * D_MODEL // RET_V_DIM
    half = RET_QK_DIM // 2
    return pl.pallas_call(
        _retention_kernel,
        grid=(b, RET_HEADS),
        in_specs=[
            pl.BlockSpec(memory_space=pltpu.SMEM),
            pl.BlockSpec((None, s, RET_QK_DIM), lambda bi, h: (bi, 0, h)),
            pl.BlockSpec((None, s, RET_QK_DIM), lambda bi, h: (bi, 0, qk_blocks + h)),
            pl.BlockSpec((None, s, RET_V_DIM), lambda bi, h: (bi, 0, v_blocks + h)),
            pl.BlockSpec((None, s, RET_V_DIM), lambda bi, h: (bi, 0, 2 * v_blocks + h)),
            pl.BlockSpec((s, half), lambda bi, h: (0, 0)),
            pl.BlockSpec((s, half), lambda bi, h: (0, 0)),
        ],
        out_specs=pl.BlockSpec((None, s, RET_V_DIM), lambda bi, h: (bi, 0, h)),
        out_shape=jax.ShapeDtypeStruct((b, s, RET_HEADS * RET_V_DIM), BF16),
        scratch_shapes=[pltpu.VMEM((RET_QK_DIM, RET_V_DIM), F32)],
        compiler_params=_params("parallel", "parallel"),
        name="retention",
    )(log_gamma, proj, proj, proj, proj, cos, sin)


def _sgu_kernel(z_ref, x_ref, nrm_ref, ws_ref, bs_ref, wo_ref, o_ref, vn_ref, um_ref):
    rows = z_ref.shape[0]
    v = z_ref[:, SGU_WIDTH:].astype(F32)
    vn_ref[...] = _rmsnorm(v, nrm_ref[...]).astype(BF16)
    r = lax.broadcasted_iota(jnp.int32, (BLOCK, BLOCK), 0)
    c = lax.broadcasted_iota(jnp.int32, (BLOCK, BLOCK), 1)
    causal = c <= r
    for g in range(SGU_GROUPS):
        w = jnp.where(causal, ws_ref[g], 0.0).astype(BF16)
        bias = bs_ref[g]
        cols = slice(g * SGU_GROUP_DIM, (g + 1) * SGU_GROUP_DIM)
        for n in range(rows // BLOCK):
            rs = slice(n * BLOCK, (n + 1) * BLOCK)
            mixed = _dot(w, vn_ref[rs, cols]) + bias
            um_ref[rs, cols] = (z_ref[rs, cols].astype(F32) * mixed).astype(BF16)
    o_ref[...] = x_ref[...] + _dot(um_ref[...], wo_ref[...])


def _sgu(z, x, sgu_norm, w_s, b_s, w_out, *, row_tile=512):
    rows, d = x.shape
    return pl.pallas_call(
        _sgu_kernel,
        grid=(rows // row_tile,),
        in_specs=[
            pl.BlockSpec((row_tile, 2 * SGU_WIDTH), lambda i: (i, 0)),
            pl.BlockSpec((row_tile, d), lambda i: (i, 0)),
            pl.BlockSpec((1, SGU_WIDTH), lambda i: (0, 0)),
            pl.BlockSpec((SGU_GROUPS, BLOCK, BLOCK), lambda i: (0, 0, 0)),
            pl.BlockSpec((SGU_GROUPS, BLOCK, 1), lambda i: (0, 0, 0)),
            pl.BlockSpec((SGU_WIDTH, d), lambda i: (0, 0)),
        ],
        out_specs=pl.BlockSpec((row_tile, d), lambda i: (i, 0)),
        out_shape=jax.ShapeDtypeStruct((rows, d), F32),
        scratch_shapes=[pltpu.VMEM((row_tile, SGU_WIDTH), BF16), pltpu.VMEM((row_tile, SGU_WIDTH), BF16)],
        compiler_params=_params("parallel"),
        name="sgu",
    )(z, x, sgu_norm.reshape(1, SGU_WIDTH), w_s, b_s.reshape(SGU_GROUPS, BLOCK, 1), w_out)


def _ffn_kernel(x_ref, g_ref, wg_ref, wu_ref, cw_ref, cb_ref, wo_ref, o_ref,
                h_ref, acc_ref, carry_ref, shift_ref, *, tiles_per_seq):
    i = pl.program_id(0)
    j = pl.program_id(1)
    rows = x_ref.shape[0]
    halo = SUBLANES

    @pl.when(j == 0)
    def _():
        h_ref[...] = _rmsnorm(x_ref[...], g_ref[...]).astype(BF16)
        acc_ref[...] = jnp.zeros_like(acc_ref)

    h = h_ref[...]
    gate = _dot(h, wg_ref[...])
    up = _dot(h, wu_ref[...])
    seq_start = (i % tiles_per_seq) == 0
    shift_ref[0:halo, :] = jnp.where(seq_start, 0.0, carry_ref[j])
    shift_ref[halo:, :] = gate
    carry_ref[j] = gate[rows - halo:, :]
    g1 = shift_ref[halo - 1:halo - 1 + rows, :]
    g2 = shift_ref[halo - 2:halo - 2 + rows, :]
    conv = cb_ref[...] + cw_ref[0:1, :] * g2 + cw_ref[1:2, :] * g1 + cw_ref[2:3, :] * gate
    act = (conv * jax.nn.sigmoid(conv) * up).astype(BF16)
    acc_ref[...] += _dot(act, wo_ref[...])

    @pl.when(j == pl.num_programs(1) - 1)
    def _():
        o_ref[...] = x_ref[...] + acc_ref[...]


def _ffn(x, g, w_in, conv_w, conv_b, w_out, *, seq, f_tile=256):
    rows, d = x.shape
    nf = FFN_DIM // f_tile
    return pl.pallas_call(
        functools.partial(_ffn_kernel, tiles_per_seq=seq // ROW_TILE),
        grid=(rows // ROW_TILE, nf),
        in_specs=[
            pl.BlockSpec((ROW_TILE, d), lambda i, j: (i, 0)),
            pl.BlockSpec((1, d), lambda i, j: (0, 0)),
            pl.BlockSpec((d, f_tile), lambda i, j: (0, j)),
            pl.BlockSpec((d, f_tile), lambda i, j: (0, nf + j)),
            pl.BlockSpec((CONV_WIDTH, f_tile), lambda i, j: (0, j)),
            pl.BlockSpec((1, f_tile), lambda i, j: (0, j)),
            pl.BlockSpec((f_tile, d), lambda i, j: (j, 0)),
        ],
        out_specs=pl.BlockSpec((ROW_TILE, d), lambda i, j: (i, 0)),
        out_shape=jax.ShapeDtypeStruct((rows, d), F32),
        scratch_shapes=[
            pltpu.VMEM((ROW_TILE, d), BF16),
            pltpu.VMEM((ROW_TILE, d), F32),
            pltpu.VMEM((nf, SUBLANES, f_tile), F32),
            pltpu.VMEM((SUBLANES + ROW_TILE, f_tile), F32),
        ],
        compiler_params=_params("arbitrary", "arbitrary"),
        name="conv_ffn",
    )(x, g.reshape(1, d), w_in, w_in, conv_w, conv_b.reshape(1, FFN_DIM), w_out)


def _ple_kernel(x_ref, p_ref, g_ref, wg_ref, wp_ref, gf_ref, o_ref, *, final):
    x = x_ref[...]
    h = _rmsnorm(x, g_ref[...]).astype(BF16)
    gate = jax.nn.sigmoid(_dot(h, wg_ref[...]))
    y = x + gate * _dot(p_ref[...].astype(BF16), wp_ref[...])
    if final:
        y = _rmsnorm(y, gf_ref[...])
    o_ref[...] = y


def _ple(x, p, layer, g, w_gate, w_proj, g_final, *, final):
    rows, d = x.shape
    return pl.pallas_call(
        functools.partial(_ple_kernel, final=final),
        grid=(rows // ROW_TILE,),
        in_specs=[
            pl.BlockSpec((ROW_TILE, d), lambda i: (i, 0)),
            pl.BlockSpec((None, ROW_TILE, PL_DIM), lambda i: (layer, i, 0)),
            pl.BlockSpec((1, d), lambda i: (0, 0)),
            pl.BlockSpec((d, d), lambda i: (0, 0)),
            pl.BlockSpec((PL_DIM, d), lambda i: (0, 0)),
            pl.BlockSpec((1, d), lambda i: (0, 0)),
        ],
        out_specs=pl.BlockSpec((ROW_TILE, d), lambda i: (i, 0)),
        out_shape=jax.ShapeDtypeStruct((rows, d), F32),
        compiler_params=_params("parallel"),
        name="per_layer_embed",
    )(x, p, g.reshape(1, d), w_gate, w_proj, g_final.reshape(1, d))


def _rotary_tables(seq):
    half = RET_QK_DIM // 2
    inv_freq = 1.0 / (ROPE_BASE ** (jnp.arange(half, dtype=F32) / half))
    ang = jnp.arange(seq, dtype=F32)[:, None] * inv_freq[None, :]
    return jnp.cos(ang), jnp.sin(ang)


def kernel(x, p, norm_mix, norm_ffn, norm_pl, norm_final, sb_w_in, sb_w_out, ret_w_in, ret_w_out, sgu_w_in, sgu_norm, sgu_w_s, sgu_b_s, sgu_w_out, ffn_w_in, ffn_conv_w, ffn_conv_b, ffn_w_out, pl_w_gate, pl_w_proj):
    b, s, d = x.shape
    rows = b * s
    xf = x.reshape(rows, d)
    pf = p.reshape(DEPTH, rows, PL_DIM)
    log_gamma = jnp.log1p(-jnp.exp2(-5.0 - jnp.arange(RET_HEADS, dtype=F32)))
    cos, sin = _rotary_tables(s)
    for i in range(DEPTH):
        kind = i % N_MIXERS
        j = i // N_MIXERS
        if kind == 0:
            qkv = _norm_matmul(xf, norm_mix[i], sb_w_in[j].astype(BF16))
            o = _sb_attention(qkv.reshape(b, s, 3 * d))
            xf = _matmul_residual(o.reshape(rows, d), sb_w_out[j].astype(BF16), xf)
        elif kind == 1:
            proj = _norm_matmul(xf, norm_mix[i], ret_w_in[j].astype(BF16))
            y = _retention(proj.reshape(b, s, 6 * d), log_gamma, cos, sin)
            xf = _matmul_residual(y.reshape(rows, 2 * d), ret_w_out[j].astype(BF16), xf)
        else:
            z = _norm_matmul(xf, norm_mix[i], sgu_w_in[j].astype(BF16), gelu=True)
            xf = _sgu(z, xf, sgu_norm[j], sgu_w_s[j], sgu_b_s[j], sgu_w_out[j].astype(BF16))
        xf = _ffn(xf, norm_ffn[i], ffn_w_in[i].astype(BF16), ffn_conv_w[i], ffn_conv_b[i],
                  ffn_w_out[i].astype(BF16), seq=s)
        xf = _ple(xf, pf, i, norm_pl[i], pl_w_gate[i].astype(BF16), pl_w_proj[i].astype(BF16),
                  norm_final, final=(i == DEPTH - 1))
    return xf.reshape(b, s, d)
```

```python
import functools

import jax
import jax.numpy as jnp
from jax import lax
from jax.experimental import pallas as pl
from jax.experimental.pallas import tpu as pltpu

D_MODEL = 1024
DEPTH = 4
N_MIXERS = 3
PL_DIM = 256
EPS = 1e-6
BLOCK = 128

SB_HEADS = 16
SB_HEAD_DIM = D_MODEL // SB_HEADS
RET_HEADS = 4
RET_QK_DIM = D_MODEL // RET_HEADS
RET_V_DIM = 2 * D_MODEL // RET_HEADS
ROPE_BASE = 10000.0
SGU_WIDTH = 2 * D_MODEL
SGU_GROUPS = 8
SGU_GROUP_DIM = SGU_WIDTH // SGU_GROUPS
FFN_DIM = 2816
CONV_WIDTH = 3

LANES = 128
SUBLANES = 8
MXU_DIM = 256
VMEM_LIMIT_BYTES = 56 * 1024 * 1024

ROW_TILE = 1024
FFN_ROW_TILE = 512
FFN_CHUNK = MXU_DIM
SB_BLOCK = MXU_DIM
LOG2E = 1.4426950408889634

F32 = jnp.float32
BF16 = jnp.bfloat16


def _params(*semantics):
    return pltpu.CompilerParams(dimension_semantics=semantics, vmem_limit_bytes=VMEM_LIMIT_BYTES)


def _rmsnorm(x, g):
    return x * lax.rsqrt(jnp.mean(x * x, axis=-1, keepdims=True) + EPS) * g


def _dot(a, b):
    return jnp.dot(a, b, preferred_element_type=F32)


def _dot_nt(a, b):
    return lax.dot_general(a, b, (((1,), (1,)), ((), ())), preferred_element_type=F32)


def _norm_matmul_kernel(x_ref, g_ref, w_ref, o_ref, h_ref, *, gelu):
    @pl.when(pl.program_id(1) == 0)
    def _():
        h_ref[...] = _rmsnorm(x_ref[...], g_ref[...]).astype(BF16)

    acc = _dot(h_ref[...], w_ref[...])
    if gelu:
        acc = jax.nn.gelu(acc)
    o_ref[...] = acc.astype(o_ref.dtype)


def _norm_matmul(x, g, w, *, gelu=False, col_tile=1024):
    rows, d = x.shape
    n = w.shape[1]
    return pl.pallas_call(
        functools.partial(_norm_matmul_kernel, gelu=gelu),
        grid=(rows // ROW_TILE, n // col_tile),
        in_specs=[
            pl.BlockSpec((ROW_TILE, d), lambda i, j: (i, 0)),
            pl.BlockSpec((1, d), lambda i, j: (0, 0)),
            pl.BlockSpec((d, col_tile), lambda i, j: (0, j)),
        ],
        out_specs=pl.BlockSpec((ROW_TILE, col_tile), lambda i, j: (i, j)),
        out_shape=jax.ShapeDtypeStruct((rows, n), BF16),
        scratch_shapes=[pltpu.VMEM((ROW_TILE, d), BF16)],
        compiler_params=_params("parallel", "arbitrary"),
        name="norm_matmul",
    )(x, g.reshape(1, d), w)


def _matmul_residual_kernel(a_ref, w_ref, x_ref, o_ref):
    o_ref[...] = x_ref[...] + _dot(a_ref[...], w_ref[...])


def _matmul_residual(a, w, x):
    rows, k = a.shape
    d = w.shape[1]
    return pl.pallas_call(
        _matmul_residual_kernel,
        grid=(rows // ROW_TILE,),
        in_specs=[
            pl.BlockSpec((ROW_TILE, k), lambda i: (i, 0)),
            pl.BlockSpec((k, d), lambda i: (0, 0)),
            pl.BlockSpec((ROW_TILE, d), lambda i: (i, 0)),
        ],
        out_specs=pl.BlockSpec((ROW_TILE, d), lambda i: (i, 0)),
        out_shape=jax.ShapeDtypeStruct((rows, d), F32),
        compiler_params=_params("parallel"),
        name="matmul_residual",
    )(a, w, x)


def _sb_scores(qm, kb, mask):
    z = _dot_nt(qm, kb)
    e = jnp.exp2(jnp.abs(z) * (-LOG2E))
    nlk = jnp.log(1.0 + e) + jnp.maximum(z, 0.0)
    if mask is not None:
        nlk = jnp.where(mask, nlk, 0.0)
    hi = nlk.astype(BF16)
    lo = (nlk - hi.astype(F32)).astype(BF16)
    return z, jnp.concatenate([hi, lo], axis=1)


def _sb_weighted_values(z, hilo, vb, tri2, carry, mask):
    neg_suffix = _dot(hilo, tri2)
    a = jnp.exp2((neg_suffix + (z - carry)) * LOG2E)
    if mask is not None:
        a = jnp.where(mask, a, 0.0)
    return _dot(a.astype(BF16), vb), carry - neg_suffix[:, 0:1]


def _sb_attention_kernel(q_ref, k_ref, v_ref, o_ref, qs_ref, acc_ref, car_ref):
    seq = q_ref.shape[0]
    blk = SB_BLOCK
    sup = 2 * blk
    first_head = lax.broadcasted_iota(jnp.int32, (blk, LANES), 1) < SB_HEAD_DIM
    r = lax.broadcasted_iota(jnp.int32, (2 * blk, blk), 0) & (blk - 1)
    c = lax.broadcasted_iota(jnp.int32, (2 * blk, blk), 1)
    tri2 = jnp.where(r >= c, -1.0, 0.0).astype(BF16)
    diag_mask = c < r
    scale = jnp.asarray(SB_HEAD_DIM ** -0.5, BF16)

    def stack_heads(q):
        zero = jnp.zeros_like(q)
        return jnp.concatenate([jnp.where(first_head, q, zero), jnp.where(first_head, zero, q)], axis=0)

    def merge_heads(acc):
        return jnp.where(first_head, acc[:blk], acc[blk:]).astype(o_ref.dtype)

    def super_block(m, _):
        r0 = pl.multiple_of(m * sup, sup)
        lo_rows, hi_rows = pl.ds(r0, blk), pl.ds(r0 + blk, blk)
        q_lo = stack_heads(q_ref[lo_rows, :] * scale)
        q_hi = stack_heads(q_ref[hi_rows, :] * scale)
        qs_ref[...] = jnp.concatenate([q_lo, q_hi], axis=0)
        z_ll, s_ll = _sb_scores(q_lo, k_ref[lo_rows, :], diag_mask)
        z_hh, s_hh = _sb_scores(q_hi, k_ref[hi_rows, :], diag_mask)
        z_hl, s_hl = _sb_scores(q_hi, k_ref[lo_rows, :], None)
        no_carry = jnp.zeros((2 * blk, 1), F32)
        o_ll, c_lo = _sb_weighted_values(z_ll, s_ll, v_ref[lo_rows, :], tri2, no_carry, diag_mask)
        o_hh, c_hi = _sb_weighted_values(z_hh, s_hh, v_ref[hi_rows, :], tri2, no_carry, diag_mask)
        o_hl, c_hi = _sb_weighted_values(z_hl, s_hl, v_ref[lo_rows, :], tri2, c_hi, None)
        acc_ref[...] = jnp.concatenate([o_ll, o_hh + o_hl], axis=0)
        car_ref[...] = jnp.concatenate([c_lo, c_hi], axis=0)

        def key_super(jj, _):
            s0 = pl.multiple_of((m - 1 - jj) * sup, sup)
            right, left = pl.ds(s0 + blk, blk), pl.ds(s0, blk)
            qs = qs_ref[...]
            z_r, s_r = _sb_scores(qs, k_ref[right, :], None)
            z_l, s_l = _sb_scores(qs, k_ref[left, :], None)
            o_r, car = _sb_weighted_values(z_r, s_r, v_ref[right, :], tri2, car_ref[...], None)
            o_l, car = _sb_weighted_values(z_l, s_l, v_ref[left, :], tri2, car, None)
            car_ref[...] = car
            acc_ref[...] += o_r + o_l
            return 0

        lax.fori_loop(0, m, key_super, 0)
        o_ref[lo_rows, :] = merge_heads(acc_ref[0:2 * blk, :])
        o_ref[hi_rows, :] = merge_heads(acc_ref[2 * blk:4 * blk, :])
        return 0

    lax.fori_loop(0, seq // sup, super_block, 0)


def _sb_attention(qkv):
    b, s, _ = qkv.shape
    col_blocks = D_MODEL // LANES
    chains = 4 * SB_BLOCK
    return pl.pallas_call(
        _sb_attention_kernel,
        grid=(b, col_blocks),
        in_specs=[
            pl.BlockSpec((None, s, LANES), lambda bi, c: (bi, 0, c)),
            pl.BlockSpec((None, s, LANES), lambda bi, c: (bi, 0, col_blocks + c)),
            pl.BlockSpec((None, s, LANES), lambda bi, c: (bi, 0, 2 * col_blocks + c)),
        ],
        out_specs=pl.BlockSpec((None, s, LANES), lambda bi, c: (bi, 0, c)),
        out_shape=jax.ShapeDtypeStruct((b, s, D_MODEL), BF16),
        scratch_shapes=[
            pltpu.VMEM((chains, LANES), BF16),
            pltpu.VMEM((chains, LANES), F32),
            pltpu.VMEM((chains, 1), F32),
        ],
        compiler_params=_params("parallel", "parallel"),
        name="sb_attention",
    )(qkv, qkv, qkv)


def _rotate(x, cos, sin):
    half = x.shape[-1] // 2
    x1, x2 = x[:, :half], x[:, half:]
    return jnp.concatenate([x1 * cos - x2 * sin, x2 * cos + x1 * sin], axis=-1)


def _retention_kernel(lg_ref, q_ref, k_ref, v_ref, g_ref, cos_ref, sin_ref, o_ref, state_ref):
    seq = q_ref.shape[0]
    lg = lg_ref[pl.program_id(1)]
    r = lax.broadcasted_iota(jnp.int32, (BLOCK, BLOCK), 0)
    c = lax.broadcasted_iota(jnp.int32, (BLOCK, BLOCK), 1)
    rel = (r - c).astype(F32)
    decay_intra = jnp.where(rel >= 0, jnp.exp(lg * jnp.maximum(rel, 0.0)), 0.0)
    pos_k = lax.broadcasted_iota(jnp.int32, (BLOCK, RET_QK_DIM), 0).astype(F32)
    decay_k = jnp.exp(lg * (BLOCK - 1.0 - pos_k))
    pos_v = lax.broadcasted_iota(jnp.int32, (BLOCK, RET_V_DIM), 0).astype(F32)
    decay_q = jnp.exp(lg * (pos_v + 1.0))
    decay_chunk = jnp.exp(jnp.full((1, RET_V_DIM), lg * BLOCK, F32))
    state_ref[...] = jnp.zeros_like(state_ref)

    def chunk(n, _):
        r0 = pl.multiple_of(n * BLOCK, BLOCK)
        rows = pl.ds(r0, BLOCK)
        cos = cos_ref[rows, :]
        sin = sin_ref[rows, :]
        q = _rotate(q_ref[rows, :].astype(F32), cos, sin)
        k = _rotate(k_ref[rows, :].astype(F32), cos, sin) * (RET_QK_DIM ** -0.5)
        v = v_ref[rows, :]
        qb = q.astype(BF16)
        scores = _dot_nt(qb, k.astype(BF16)) * decay_intra
        intra = _dot(scores.astype(BF16), v)
        state = state_ref[...]
        inter = _dot(qb, state.astype(BF16))
        k_dec_t = (k * decay_k).T.astype(BF16)
        state_ref[...] = state * decay_chunk + _dot(k_dec_t, v)
        o = intra + inter * decay_q
        mu = jnp.mean(o, axis=-1, keepdims=True)
        oc = o - mu
        var = jnp.mean(oc * oc, axis=-1, keepdims=True)
        on = oc * lax.rsqrt(var + EPS)
        gate = g_ref[rows, :].astype(F32)
        o_ref[rows, :] = (gate * jax.nn.sigmoid(gate) * on).astype(o_ref.dtype)
        return 0

    lax.fori_loop(0, seq // BLOCK, chunk, 0)


def _retention(proj, log_gamma, cos, sin):
    b, s, _ = proj.shape
    qk_blocks = D_MODEL // RET_QK_DIM
    v_blocks = 2 * D_MODEL // RET_V_DIM
    half = RET_QK_DIM // 2
    return pl.pallas_call(
        _retention_kernel,
        grid=(b, RET_HEADS),
        in_specs=[
            pl.BlockSpec(memory_space=pltpu.SMEM),
            pl.BlockSpec((None, s, RET_QK_DIM), lambda bi, h: (bi, 0, h)),
            pl.BlockSpec((None, s, RET_QK_DIM), lambda bi, h: (bi, 0, qk_blocks + h)),
            pl.BlockSpec((None, s, RET_V_DIM), lambda bi, h: (bi, 0, v_blocks + h)),
            pl.BlockSpec((None, s, RET_V_DIM), lambda bi, h: (bi, 0, 2 * v_blocks + h)),
            pl.BlockSpec((s, half), lambda bi, h: (0, 0)),
            pl.BlockSpec((s, half), lambda bi, h: (0, 0)),
        ],
        out_specs=pl.BlockSpec((None, s, RET_V_DIM), lambda bi, h: (bi, 0, h)),
        out_shape=jax.ShapeDtypeStruct((b, s, RET_HEADS * RET_V_DIM), BF16),
        scratch_shapes=[pltpu.VMEM((RET_QK_DIM, RET_V_DIM), F32)],
        compiler_params=_params("parallel", "parallel"),
        name="retention",
    )(log_gamma, proj, proj, proj, proj, cos, sin)


def _sgu_kernel(z_ref, x_ref, nrm_ref, ws_ref, bs_ref, wo_ref, o_ref, vn_ref, um_ref):
    rows = z_ref.shape[0]
    v = z_ref[:, SGU_WIDTH:].astype(F32)
    vn_ref[...] = _rmsnorm(v, nrm_ref[...]).astype(BF16)
    r = lax.broadcasted_iota(jnp.int32, (BLOCK, BLOCK), 0)
    c = lax.broadcasted_iota(jnp.int32, (BLOCK, BLOCK), 1)
    causal = c <= r
    for g in range(SGU_GROUPS):
        w = jnp.where(causal, ws_ref[g], 0.0).astype(BF16)
        bias = bs_ref[g]
        cols = slice(g * SGU_GROUP_DIM, (g + 1) * SGU_GROUP_DIM)
        for n in range(rows // BLOCK):
            rs = slice(n * BLOCK, (n + 1) * BLOCK)
            mixed = _dot(w, vn_ref[rs, cols]) + bias
            um_ref[rs, cols] = (z_ref[rs, cols].astype(F32) * mixed).astype(BF16)
    o_ref[...] = x_ref[...] + _dot(um_ref[...], wo_ref[...])


def _sgu(z, x, sgu_norm, w_s, b_s, w_out, *, row_tile=512):
    rows, d = x.shape
    return pl.pallas_call(
        _sgu_kernel,
        grid=(rows // row_tile,),
        in_specs=[
            pl.BlockSpec((row_tile, 2 * SGU_WIDTH), lambda i: (i, 0)),
            pl.BlockSpec((row_tile, d), lambda i: (i, 0)),
            pl.BlockSpec((1, SGU_WIDTH), lambda i: (0, 0)),
            pl.BlockSpec((SGU_GROUPS, BLOCK, BLOCK), lambda i: (0, 0, 0)),
            pl.BlockSpec((SGU_GROUPS, BLOCK, 1), lambda i: (0, 0, 0)),
            pl.BlockSpec((SGU_WIDTH, d), lambda i: (0, 0)),
        ],
        out_specs=pl.BlockSpec((row_tile, d), lambda i: (i, 0)),
        out_shape=jax.ShapeDtypeStruct((rows, d), F32),
        scratch_shapes=[pltpu.VMEM((row_tile, SGU_WIDTH), BF16), pltpu.VMEM((row_tile, SGU_WIDTH), BF16)],
        compiler_params=_params("parallel"),
        name="sgu",
    )(z, x, sgu_norm.reshape(1, SGU_WIDTH), w_s, b_s.reshape(SGU_GROUPS, BLOCK, 1), w_out)


def _ffn_kernel(x_ref, g_ref, win_ref, cw_ref, cb_ref, wout_ref, o_ref,
                h_ref, act_ref, carry_ref, *, tiles_per_seq):
    rows = x_ref.shape[0]
    halo = SUBLANES
    h_ref[...] = _rmsnorm(x_ref[...], g_ref[...]).astype(BF16)
    seq_start = (pl.program_id(0) % tiles_per_seq) == 0
    for c in range(FFN_DIM // FFN_CHUNK):
        cols = slice(c * FFN_CHUNK, (c + 1) * FFN_CHUNK)
        up_cols = slice(FFN_DIM + c * FFN_CHUNK, FFN_DIM + (c + 1) * FFN_CHUNK)
        h = h_ref[...]
        gate = _dot(h, win_ref[:, cols])
        up = _dot(h, win_ref[:, up_cols])
        prev = jnp.where(seq_start, 0.0, carry_ref[:, cols])
        carry_ref[:, cols] = gate[rows - halo:, :]
        ext = jnp.concatenate([prev, gate], axis=0)
        g1 = pltpu.roll(ext, 1, axis=0)[halo:]
        g2 = pltpu.roll(ext, 2, axis=0)[halo:]
        conv = cb_ref[:, cols] + cw_ref[0:1, cols] * g2 + cw_ref[1:2, cols] * g1 + cw_ref[2:3, cols] * gate
        act_ref[:, cols] = (conv * jax.nn.sigmoid(conv) * up).astype(BF16)
    o_ref[...] = x_ref[...] + _dot(act_ref[...], wout_ref[...])


def _resident(shape):
    return pl.BlockSpec(shape, lambda i: (0,) * len(shape), pipeline_mode=pl.Buffered(1))


def _ffn(x, g, w_in, conv_w, conv_b, w_out, *, seq):
    rows, d = x.shape
    return pl.pallas_call(
        functools.partial(_ffn_kernel, tiles_per_seq=seq // FFN_ROW_TILE),
        grid=(rows // FFN_ROW_TILE,),
        in_specs=[
            pl.BlockSpec((FFN_ROW_TILE, d), lambda i: (i, 0)),
            _resident((1, d)),
            _resident((d, 2 * FFN_DIM)),
            _resident((CONV_WIDTH, FFN_DIM)),
            _resident((1, FFN_DIM)),
            _resident((FFN_DIM, d)),
        ],
        out_specs=pl.BlockSpec((FFN_ROW_TILE, d), lambda i: (i, 0)),
        out_shape=jax.ShapeDtypeStruct((rows, d), F32),
        scratch_shapes=[
            pltpu.VMEM((FFN_ROW_TILE, d), BF16),
            pltpu.VMEM((FFN_ROW_TILE, FFN_DIM), BF16),
            pltpu.VMEM((SUBLANES, FFN_DIM), F32),
        ],
        compiler_params=_params("arbitrary"),
        name="conv_ffn",
    )(x, g.reshape(1, d), w_in, conv_w, conv_b.reshape(1, FFN_DIM), w_out)


def _ple_kernel(x_ref, p_ref, g_ref, wg_ref, wp_ref, gf_ref, o_ref, *, final):
    x = x_ref[...]
    h = _rmsnorm(x, g_ref[...]).astype(BF16)
    gate = jax.nn.sigmoid(_dot(h, wg_ref[...]))
    y = x + gate * _dot(p_ref[...].astype(BF16), wp_ref[...])
    if final:
        y = _rmsnorm(y, gf_ref[...])
    o_ref[...] = y


def _ple(x, p, layer, g, w_gate, w_proj, g_final, *, final):
    rows, d = x.shape
    return pl.pallas_call(
        functools.partial(_ple_kernel, final=final),
        grid=(rows // ROW_TILE,),
        in_specs=[
            pl.BlockSpec((ROW_TILE, d), lambda i: (i, 0)),
            pl.BlockSpec((None, ROW_TILE, PL_DIM), lambda i: (layer, i, 0)),
            pl.BlockSpec((1, d), lambda i: (0, 0)),
            pl.BlockSpec((d, d), lambda i: (0, 0)),
            pl.BlockSpec((PL_DIM, d), lambda i: (0, 0)),
            pl.BlockSpec((1, d), lambda i: (0, 0)),
        ],
        out_specs=pl.BlockSpec((ROW_TILE, d), lambda i: (i, 0)),
        out_shape=jax.ShapeDtypeStruct((rows, d), F32),
        compiler_params=_params("parallel"),
        name="per_layer_embed",
    )(x, p, g.reshape(1, d), w_gate, w_proj, g_final.reshape(1, d))


def _rotary_tables(seq):
    half = RET_QK_DIM // 2
    inv_freq = 1.0 / (ROPE_BASE ** (jnp.arange(half, dtype=F32) / half))
    ang = jnp.arange(seq, dtype=F32)[:, None] * inv_freq[None, :]
    return jnp.cos(ang), jnp.sin(ang)


def kernel(x, p, norm_mix, norm_ffn, norm_pl, norm_final, sb_w_in, sb_w_out, ret_w_in, ret_w_out, sgu_w_in, sgu_norm, sgu_w_s, sgu_b_s, sgu_w_out, ffn_w_in, ffn_conv_w, ffn_conv_b, ffn_w_out, pl_w_gate, pl_w_proj):
    b, s, d = x.shape
    rows = b * s
    xf = x.reshape(rows, d)
    pf = p.reshape(DEPTH, rows, PL_DIM)
    log_gamma = jnp.log1p(-jnp.exp2(-5.0 - jnp.arange(RET_HEADS, dtype=F32)))
    cos, sin = _rotary_tables(s)
    for i in range(DEPTH):
        kind = i % N_MIXERS
        j = i // N_MIXERS
        if kind == 0:
            qkv = _norm_matmul(xf, norm_mix[i], sb_w_in[j].astype(BF16))
            o = _sb_attention(qkv.reshape(b, s, 3 * d))
            xf = _matmul_residual(o.reshape(rows, d), sb_w_out[j].astype(BF16), xf)
        elif kind == 1:
            proj = _norm_matmul(xf, norm_mix[i], ret_w_in[j].astype(BF16))
            y = _retention(proj.reshape(b, s, 6 * d), log_gamma, cos, sin)
            xf = _matmul_residual(y.reshape(rows, 2 * d), ret_w_out[j].astype(BF16), xf)
        else:
            z = _norm_matmul(xf, norm_mix[i], sgu_w_in[j].astype(BF16), gelu=True)
            xf = _sgu(z, xf, sgu_norm[j], sgu_w_s[j], sgu_b_s[j], sgu_w_out[j].astype(BF16))
        xf = _ffn(xf, norm_ffn[i], ffn_w_in[i].astype(BF16), ffn_conv_w[i], ffn_conv_b[i],
                  ffn_w_out[i].astype(BF16), seq=s)
        xf = _ple(xf, pf, i, norm_pl[i], pl_w_gate[i].astype(BF16), pl_w_proj[i].astype(BF16),
                  norm_final, final=(i == DEPTH - 1))
    return xf.reshape(b, s, d)
```

```python
import functools

import jax
import jax.numpy as jnp
from jax import lax
from jax.experimental import pallas as pl
from jax.experimental.pallas import tpu as pltpu

D_MODEL = 1024
DEPTH = 4
N_MIXERS = 3
PL_DIM = 256
EPS = 1e-6
BLOCK = 128

SB_HEADS = 16
SB_HEAD_DIM = D_MODEL // SB_HEADS
RET_HEADS = 4
RET_QK_DIM = D_MODEL // RET_HEADS
RET_V_DIM = 2 * D_MODEL // RET_HEADS
ROPE_BASE = 10000.0
SGU_WIDTH = 2 * D_MODEL
SGU_GROUPS = 8
SGU_GROUP_DIM = SGU_WIDTH // SGU_GROUPS
FFN_DIM = 2816
CONV_WIDTH = 3

LANES = 128
SUBLANES = 8
MXU_DIM = 256
VMEM_LIMIT_BYTES = 56 * 1024 * 1024

ROW_TILE = 512
PROJ_CHUNK = 512
FFN_CHUNK = MXU_DIM
SB_BLOCK = MXU_DIM
LOG2E = 1.4426950408889634

F32 = jnp.float32
BF16 = jnp.bfloat16


def _params(*semantics):
    return pltpu.CompilerParams(dimension_semantics=semantics, vmem_limit_bytes=VMEM_LIMIT_BYTES)


def _rmsnorm(x, g):
    return x * lax.rsqrt(jnp.mean(x * x, axis=-1, keepdims=True) + EPS) * g


def _dot(a, b):
    return jnp.dot(a, b, preferred_element_type=F32)


def _dot_nt(a, b):
    return lax.dot_general(a, b, (((1,), (1,)), ((), ())), preferred_element_type=F32)


def _resident(shape):
    return pl.BlockSpec(shape, lambda i: (0,) * len(shape), pipeline_mode=pl.Buffered(1))


def _norm_matmul_kernel(x_ref, g_ref, w_ref, o_ref, h_ref, *, gelu):
    h_ref[...] = _rmsnorm(x_ref[...], g_ref[...]).astype(BF16)
    for c in range(w_ref.shape[1] // PROJ_CHUNK):
        cols = slice(c * PROJ_CHUNK, (c + 1) * PROJ_CHUNK)
        acc = _dot(h_ref[...], w_ref[:, cols])
        if gelu:
            acc = jax.nn.gelu(acc)
        o_ref[:, cols] = acc.astype(o_ref.dtype)


def _norm_matmul(x, g, w, *, gelu=False):
    rows, d = x.shape
    n = w.shape[1]
    return pl.pallas_call(
        functools.partial(_norm_matmul_kernel, gelu=gelu),
        grid=(rows // ROW_TILE,),
        in_specs=[
            pl.BlockSpec((ROW_TILE, d), lambda i: (i, 0)),
            _resident((1, d)),
            _resident((d, n)),
        ],
        out_specs=pl.BlockSpec((ROW_TILE, n), lambda i: (i, 0)),
        out_shape=jax.ShapeDtypeStruct((rows, n), BF16),
        scratch_shapes=[pltpu.VMEM((ROW_TILE, d), BF16)],
        compiler_params=_params("parallel"),
        name="norm_matmul",
    )(x, g.reshape(1, d), w)


def _sb_scores(qm, kb, mask):
    z = _dot_nt(qm, kb)
    e = jnp.exp2(jnp.abs(z) * (-LOG2E))
    nlk = jnp.log(1.0 + e) + jnp.maximum(z, 0.0)
    if mask is not None:
        nlk = jnp.where(mask, nlk, 0.0)
    hi = nlk.astype(BF16)
    lo = (nlk - hi.astype(F32)).astype(BF16)
    return z, jnp.concatenate([hi, lo], axis=1)


def _sb_weighted_values(z, hilo, vb, tri2, carry, mask):
    neg_suffix = _dot(hilo, tri2)
    a = jnp.exp2((neg_suffix + (z - carry)) * LOG2E)
    if mask is not None:
        a = jnp.where(mask, a, 0.0)
    return _dot(a.astype(BF16), vb), carry - neg_suffix[:, 0:1]


def _sb_attention_kernel(q_ref, k_ref, v_ref, o_ref, qs_ref, acc_ref, car_ref):
    seq = q_ref.shape[0]
    blk = SB_BLOCK
    sup = 2 * blk
    first_head = lax.broadcasted_iota(jnp.int32, (blk, LANES), 1) < SB_HEAD_DIM
    r = lax.broadcasted_iota(jnp.int32, (2 * blk, blk), 0) & (blk - 1)
    c = lax.broadcasted_iota(jnp.int32, (2 * blk, blk), 1)
    tri2 = jnp.where(r >= c, -1.0, 0.0).astype(BF16)
    diag_mask = c < r
    scale = jnp.asarray(SB_HEAD_DIM ** -0.5, BF16)

    def stack_heads(q):
        zero = jnp.zeros_like(q)
        return jnp.concatenate([jnp.where(first_head, q, zero), jnp.where(first_head, zero, q)], axis=0)

    def merge_heads(acc):
        return jnp.where(first_head, acc[:blk], acc[blk:]).astype(o_ref.dtype)

    def super_block(m, _):
        r0 = m * sup
        lo_rows, hi_rows = pl.ds(r0, blk), pl.ds(r0 + blk, blk)
        q_lo = stack_heads(q_ref[lo_rows, :] * scale)
        q_hi = stack_heads(q_ref[hi_rows, :] * scale)
        qs_ref[...] = jnp.concatenate([q_lo, q_hi], axis=0)
        z_ll, s_ll = _sb_scores(q_lo, k_ref[lo_rows, :], diag_mask)
        z_hh, s_hh = _sb_scores(q_hi, k_ref[hi_rows, :], diag_mask)
        z_hl, s_hl = _sb_scores(q_hi, k_ref[lo_rows, :], None)
        no_carry = jnp.zeros((2 * blk, 1), F32)
        o_ll, c_lo = _sb_weighted_values(z_ll, s_ll, v_ref[lo_rows, :], tri2, no_carry, diag_mask)
        o_hh, c_hi = _sb_weighted_values(z_hh, s_hh, v_ref[hi_rows, :], tri2, no_carry, diag_mask)
        o_hl, c_hi = _sb_weighted_values(z_hl, s_hl, v_ref[lo_rows, :], tri2, c_hi, None)
        acc_ref[...] = jnp.concatenate([o_ll, o_hh + o_hl], axis=0)
        car_ref[...] = jnp.concatenate([c_lo, c_hi], axis=0)

        def key_super(jj, _):
            s0 = (m - 1 - jj) * sup
            right, left = pl.ds(s0 + blk, blk), pl.ds(s0, blk)
            qs = qs_ref[...]
            z_r, s_r = _sb_scores(qs, k_ref[right, :], None)
            z_l, s_l = _sb_scores(qs, k_ref[left, :], None)
            o_r, car = _sb_weighted_values(z_r, s_r, v_ref[right, :], tri2, car_ref[...], None)
            o_l, car = _sb_weighted_values(z_l, s_l, v_ref[left, :], tri2, car, None)
            car_ref[...] = car
            acc_ref[...] += o_r + o_l
            return 0

        for jj in range(m):
            key_super(jj, 0)
        o_ref[lo_rows, :] = merge_heads(acc_ref[0:2 * blk, :])
        o_ref[hi_rows, :] = merge_heads(acc_ref[2 * blk:4 * blk, :])
        return 0

    for m in range(seq // sup):
        super_block(m, 0)


def _sb_attention(qkv):
    b, s, _ = qkv.shape
    col_blocks = D_MODEL // LANES
    chains = 4 * SB_BLOCK
    return pl.pallas_call(
        _sb_attention_kernel,
        grid=(b, col_blocks),
        in_specs=[
            pl.BlockSpec((None, s, LANES), lambda bi, c: (bi, 0, c)),
            pl.BlockSpec((None, s, LANES), lambda bi, c: (bi, 0, col_blocks + c)),
            pl.BlockSpec((None, s, LANES), lambda bi, c: (bi, 0, 2 * col_blocks + c)),
        ],
        out_specs=pl.BlockSpec((None, s, LANES), lambda bi, c: (bi, 0, c)),
        out_shape=jax.ShapeDtypeStruct((b, s, D_MODEL), BF16),
        scratch_shapes=[
            pltpu.VMEM((chains, LANES), BF16),
            pltpu.VMEM((chains, LANES), F32),
            pltpu.VMEM((chains, 1), F32),
        ],
        compiler_params=_params("parallel", "parallel"),
        name="sb_attention",
    )(qkv, qkv, qkv)


def _rotate(x, cos, sin):
    half = x.shape[-1] // 2
    x1, x2 = x[:, :half], x[:, half:]
    return jnp.concatenate([x1 * cos - x2 * sin, x2 * cos + x1 * sin], axis=-1)


def _retention_kernel(lg_ref, q_ref, k_ref, v_ref, g_ref, cos_ref, sin_ref, o_ref, state_ref):
    seq = q_ref.shape[0]
    lg = lg_ref[pl.program_id(1)]
    r = lax.broadcasted_iota(jnp.int32, (BLOCK, BLOCK), 0)
    c = lax.broadcasted_iota(jnp.int32, (BLOCK, BLOCK), 1)
    rel = (r - c).astype(F32)
    decay_intra = jnp.where(rel >= 0, jnp.exp(lg * jnp.maximum(rel, 0.0)), 0.0)
    pos_k = lax.broadcasted_iota(jnp.int32, (BLOCK, RET_QK_DIM), 0).astype(F32)
    decay_k = jnp.exp(lg * (BLOCK - 1.0 - pos_k))
    pos_v = lax.broadcasted_iota(jnp.int32, (BLOCK, RET_V_DIM), 0).astype(F32)
    decay_q = jnp.exp(lg * (pos_v + 1.0))
    decay_chunk = jnp.exp(jnp.full((1, RET_V_DIM), lg * BLOCK, F32))
    state_ref[...] = jnp.zeros_like(state_ref)

    def chunk(n, _):
        r0 = pl.multiple_of(n * BLOCK, BLOCK)
        rows = pl.ds(r0, BLOCK)
        cos = cos_ref[rows, :]
        sin = sin_ref[rows, :]
        q = _rotate(q_ref[rows, :].astype(F32), cos, sin)
        k = _rotate(k_ref[rows, :].astype(F32), cos, sin) * (RET_QK_DIM ** -0.5)
        v = v_ref[rows, :]
        qb = q.astype(BF16)
        scores = _dot_nt(qb, k.astype(BF16)) * decay_intra
        intra = _dot(scores.astype(BF16), v)
        state = state_ref[...]
        inter = _dot(qb, state.astype(BF16))
        k_dec_t = (k * decay_k).T.astype(BF16)
        state_ref[...] = state * decay_chunk + _dot(k_dec_t, v)
        o = intra + inter * decay_q
        mu = jnp.mean(o, axis=-1, keepdims=True)
        oc = o - mu
        var = jnp.mean(oc * oc, axis=-1, keepdims=True)
        on = oc * lax.rsqrt(var + EPS)
        gate = g_ref[rows, :].astype(F32)
        o_ref[rows, :] = (gate * jax.nn.sigmoid(gate) * on).astype(o_ref.dtype)
        return 0

    lax.fori_loop(0, seq // BLOCK, chunk, 0)


def _retention(proj, log_gamma, cos, sin):
    b, s, _ = proj.shape
    qk_blocks = D_MODEL // RET_QK_DIM
    v_blocks = 2 * D_MODEL // RET_V_DIM
    half = RET_QK_DIM // 2
    return pl.pallas_call(
        _retention_kernel,
        grid=(b, RET_HEADS),
        in_specs=[
            pl.BlockSpec(memory_space=pltpu.SMEM),
            pl.BlockSpec((None, s, RET_QK_DIM), lambda bi, h: (bi, 0, h)),
            pl.BlockSpec((None, s, RET_QK_DIM), lambda bi, h: (bi, 0, qk_blocks + h)),
            pl.BlockSpec((None, s, RET_V_DIM), lambda bi, h: (bi, 0, v_blocks + h)),
            pl.BlockSpec((None, s, RET_V_DIM), lambda bi, h: (bi, 0, 2 * v_blocks + h)),
            pl.BlockSpec((s, half), lambda bi, h: (0, 0)),
            pl.BlockSpec((s, half), lambda bi, h: (0, 0)),
        ],
        out_specs=pl.BlockSpec((None, s, RET_V_DIM), lambda bi, h: (bi, 0, h)),
        out_shape=jax.ShapeDtypeStruct((b, s, RET_HEADS * RET_V_DIM), BF16),
        scratch_shapes=[pltpu.VMEM((RET_QK_DIM, RET_V_DIM), F32)],
        compiler_params=_params("parallel", "parallel"),
        name="retention",
    )(log_gamma, proj, proj, proj, proj, cos, sin)


def _sgu_kernel(z_ref, x_ref, nrm_ref, ws_ref, bs_ref, wo_ref, o_ref, vn_ref, um_ref):
    rows = z_ref.shape[0]
    v = z_ref[:, SGU_WIDTH:].astype(F32)
    vn_ref[...] = _rmsnorm(v, nrm_ref[...]).astype(BF16)
    r = lax.broadcasted_iota(jnp.int32, (BLOCK, BLOCK), 0)
    c = lax.broadcasted_iota(jnp.int32, (BLOCK, BLOCK), 1)
    causal = c <= r
    for g in range(SGU_GROUPS):
        w = jnp.where(causal, ws_ref[g], 0.0).astype(BF16)
        bias = bs_ref[g]
        cols = slice(g * SGU_GROUP_DIM, (g + 1) * SGU_GROUP_DIM)
        for n in range(rows // BLOCK):
            rs = slice(n * BLOCK, (n + 1) * BLOCK)
            mixed = _dot(w, vn_ref[rs, cols]) + bias
            um_ref[rs, cols] = (z_ref[rs, cols].astype(F32) * mixed).astype(BF16)
    o_ref[...] = x_ref[...] + _dot(um_ref[...], wo_ref[...])


def _sgu(z, x, sgu_norm, w_s, b_s, w_out, *, row_tile=ROW_TILE):
    rows, d = x.shape
    return pl.pallas_call(
        _sgu_kernel,
        grid=(rows // row_tile,),
        in_specs=[
            pl.BlockSpec((row_tile, 2 * SGU_WIDTH), lambda i: (i, 0)),
            pl.BlockSpec((row_tile, d), lambda i: (i, 0)),
            pl.BlockSpec((1, SGU_WIDTH), lambda i: (0, 0)),
            pl.BlockSpec((SGU_GROUPS, BLOCK, BLOCK), lambda i: (0, 0, 0)),
            pl.BlockSpec((SGU_GROUPS, BLOCK, 1), lambda i: (0, 0, 0)),
            pl.BlockSpec((SGU_WIDTH, d), lambda i: (0, 0)),
        ],
        out_specs=pl.BlockSpec((row_tile, d), lambda i: (i, 0)),
        out_shape=jax.ShapeDtypeStruct((rows, d), F32),
        scratch_shapes=[pltpu.VMEM((row_tile, SGU_WIDTH), BF16), pltpu.VMEM((row_tile, SGU_WIDTH), BF16)],
        compiler_params=_params("parallel"),
        name="sgu",
    )(z, x, sgu_norm.reshape(1, SGU_WIDTH), w_s, b_s.reshape(SGU_GROUPS, BLOCK, 1), w_out)


def _layer_tail_kernel(*refs, has_mix, final, tiles_per_seq):
    if has_mix:
        a_ref, wmix_ref, *refs = refs
    (x_ref, gffn_ref, win_ref, cw_ref, cb_ref, wout_ref, p_ref, gpl_ref, wgate_ref, wproj_ref, gfin_ref,
     o_ref, res_ref, h_ref, act_ref, carry_ref) = refs
    rows = x_ref.shape[0]
    halo = SUBLANES

    x1 = x_ref[...]
    if has_mix:
        x1 = x1 + _dot(a_ref[...], wmix_ref[...])
    res_ref[...] = x1

    h_ref[...] = _rmsnorm(x1, gffn_ref[...]).astype(BF16)
    seq_start = (pl.program_id(0) % tiles_per_seq) == 0
    for c in range(FFN_DIM // FFN_CHUNK):
        cols = slice(c * FFN_CHUNK, (c + 1) * FFN_CHUNK)
        up_cols = slice(FFN_DIM + c * FFN_CHUNK, FFN_DIM + (c + 1) * FFN_CHUNK)
        h = h_ref[...]
        gate = _dot(h, win_ref[:, cols])
        up = _dot(h, win_ref[:, up_cols])
        prev = jnp.where(seq_start, 0.0, carry_ref[:, cols])
        carry_ref[:, cols] = gate[rows - halo:, :]
        ext = jnp.concatenate([prev, gate], axis=0)
        g1 = pltpu.roll(ext, 1, axis=0)[halo:]
        g2 = pltpu.roll(ext, 2, axis=0)[halo:]
        conv = cb_ref[:, cols] + cw_ref[0:1, cols] * g2 + cw_ref[1:2, cols] * g1 + cw_ref[2:3, cols] * gate
        act_ref[:, cols] = (conv * jax.nn.sigmoid(conv) * up).astype(BF16)
    x2 = res_ref[...] + _dot(act_ref[...], wout_ref[...])
    res_ref[...] = x2

    h_ref[...] = _rmsnorm(x2, gpl_ref[...]).astype(BF16)
    gate = jax.nn.sigmoid(_dot(h_ref[...], wgate_ref[...]))
    y = res_ref[...] + gate * _dot(p_ref[...].astype(BF16), wproj_ref[...])
    if final:
        y = _rmsnorm(y, gfin_ref[...])
    o_ref[...] = y


def _layer_tail(x, mix, w_mix, g_ffn, w_in, conv_w, conv_b, w_out, p, layer, g_pl, w_gate, w_proj, g_final,
                *, seq, final):
    rows, d = x.shape
    has_mix = mix is not None
    row_block = lambda width: pl.BlockSpec((ROW_TILE, width), lambda i: (i, 0))
    in_specs, args = [], []
    if has_mix:
        k = mix.shape[1]
        in_specs += [row_block(k), _resident((k, d))]
        args += [mix, w_mix]
    in_specs += [
        row_block(d),
        _resident((1, d)),
        _resident((d, 2 * FFN_DIM)),
        _resident((CONV_WIDTH, FFN_DIM)),
        _resident((1, FFN_DIM)),
        _resident((FFN_DIM, d)),
        pl.BlockSpec((None, ROW_TILE, PL_DIM), lambda i: (layer, i, 0)),
        _resident((1, d)),
        _resident((d, d)),
        _resident((PL_DIM, d)),
        _resident((1, d)),
    ]
    args += [x, g_ffn.reshape(1, d), w_in, conv_w, conv_b.reshape(1, FFN_DIM), w_out,
             p, g_pl.reshape(1, d), w_gate, w_proj, g_final.reshape(1, d)]
    return pl.pallas_call(
        functools.partial(_layer_tail_kernel, has_mix=has_mix, final=final, tiles_per_seq=seq // ROW_TILE),
        grid=(rows // ROW_TILE,),
        in_specs=in_specs,
        out_specs=row_block(d),
        out_shape=jax.ShapeDtypeStruct((rows, d), F32),
        scratch_shapes=[
            pltpu.VMEM((ROW_TILE, d), F32),
            pltpu.VMEM((ROW_TILE, d), BF16),
            pltpu.VMEM((ROW_TILE, FFN_DIM), BF16),
            pltpu.VMEM((SUBLANES, FFN_DIM), F32),
        ],
        compiler_params=_params("arbitrary"),
        name="layer_tail",
    )(*args)


def _rotary_tables(seq):
    half = RET_QK_DIM // 2
    inv_freq = 1.0 / (ROPE_BASE ** (jnp.arange(half, dtype=F32) / half))
    ang = jnp.arange(seq, dtype=F32)[:, None] * inv_freq[None, :]
    return jnp.cos(ang), jnp.sin(ang)


def kernel(x, p, norm_mix, norm_ffn, norm_pl, norm_final, sb_w_in, sb_w_out, ret_w_in, ret_w_out, sgu_w_in, sgu_norm, sgu_w_s, sgu_b_s, sgu_w_out, ffn_w_in, ffn_conv_w, ffn_conv_b, ffn_w_out, pl_w_gate, pl_w_proj):
    b, s, d = x.shape
    rows = b * s
    xf = x.reshape(rows, d)
    pf = p.reshape(DEPTH, rows, PL_DIM)
    log_gamma = jnp.log1p(-jnp.exp2(-5.0 - jnp.arange(RET_HEADS, dtype=F32)))
    cos, sin = _rotary_tables(s)
    for i in range(DEPTH):
        kind = i % N_MIXERS
        j = i // N_MIXERS
        if kind == 0:
            qkv = _norm_matmul(xf, norm_mix[i], sb_w_in[j].astype(BF16))
            mix = _sb_attention(qkv.reshape(b, s, 3 * d)).reshape(rows, d)
            w_mix = sb_w_out[j].astype(BF16)
        elif kind == 1:
            proj = _norm_matmul(xf, norm_mix[i], ret_w_in[j].astype(BF16))
            mix = _retention(proj.reshape(b, s, 6 * d), log_gamma, cos, sin).reshape(rows, 2 * d)
            w_mix = ret_w_out[j].astype(BF16)
        else:
            z = _norm_matmul(xf, norm_mix[i], sgu_w_in[j].astype(BF16), gelu=True)
            xf = _sgu(z, xf, sgu_norm[j], sgu_w_s[j], sgu_b_s[j], sgu_w_out[j].astype(BF16))
            mix = w_mix = None
        xf = _layer_tail(xf, mix, w_mix, norm_ffn[i], ffn_w_in[i].astype(BF16), ffn_conv_w[i], ffn_conv_b[i],
                         ffn_w_out[i].astype(BF16), pf, i, norm_pl[i], pl_w_gate[i].astype(BF16),
                         pl_w_proj[i].astype(BF16), norm_final, seq=s, final=(i == DEPTH - 1))
    return xf.reshape(b, s, d)
```

```python
import functools

import jax
import jax.numpy as jnp
from jax import lax
from jax.experimental import pallas as pl
from jax.experimental.pallas import tpu as pltpu

D_MODEL = 1024
DEPTH = 4
N_MIXERS = 3
PL_DIM = 256
EPS = 1e-6
BLOCK = 128

SB_HEADS = 16
SB_HEAD_DIM = D_MODEL // SB_HEADS
RET_HEADS = 4
RET_QK_DIM = D_MODEL // RET_HEADS
RET_V_DIM = 2 * D_MODEL // RET_HEADS
ROPE_BASE = 10000.0
SGU_WIDTH = 2 * D_MODEL
SGU_GROUPS = 8
SGU_GROUP_DIM = SGU_WIDTH // SGU_GROUPS
FFN_DIM = 2816
CONV_WIDTH = 3

LANES = 128
SUBLANES = 8
MXU_DIM = 256
VMEM_LIMIT_BYTES = 56 * 1024 * 1024

ROW_TILE = 512
PROJ_CHUNK = 512
FFN_CHUNK = MXU_DIM
RET_ROWS = 2 * BLOCK
SB_BLOCK = MXU_DIM
LOG2E = 1.4426950408889634

F32 = jnp.float32
BF16 = jnp.bfloat16


def _params(*semantics):
    return pltpu.CompilerParams(dimension_semantics=semantics, vmem_limit_bytes=VMEM_LIMIT_BYTES)


def _rmsnorm(x, g):
    return x * lax.rsqrt(jnp.mean(x * x, axis=-1, keepdims=True) + EPS) * g


def _dot(a, b):
    return jnp.dot(a, b, preferred_element_type=F32)


def _dot_nt(a, b):
    return lax.dot_general(a, b, (((1,), (1,)), ((), ())), preferred_element_type=F32)


def _resident(shape):
    return pl.BlockSpec(shape, lambda i: (0,) * len(shape), pipeline_mode=pl.Buffered(1))


def _norm_matmul_kernel(x_ref, g_ref, w_ref, o_ref, h_ref, *, gelu):
    h_ref[...] = _rmsnorm(x_ref[...], g_ref[...]).astype(BF16)
    for c in range(w_ref.shape[1] // PROJ_CHUNK):
        cols = slice(c * PROJ_CHUNK, (c + 1) * PROJ_CHUNK)
        acc = _dot(h_ref[...], w_ref[:, cols])
        if gelu:
            acc = jax.nn.gelu(acc)
        o_ref[:, cols] = acc.astype(o_ref.dtype)


def _norm_matmul(x, g, w, *, gelu=False):
    rows, d = x.shape
    n = w.shape[1]
    return pl.pallas_call(
        functools.partial(_norm_matmul_kernel, gelu=gelu),
        grid=(rows // ROW_TILE,),
        in_specs=[
            pl.BlockSpec((ROW_TILE, d), lambda i: (i, 0)),
            _resident((1, d)),
            _resident((d, n)),
        ],
        out_specs=pl.BlockSpec((ROW_TILE, n), lambda i: (i, 0)),
        out_shape=jax.ShapeDtypeStruct((rows, n), BF16),
        scratch_shapes=[pltpu.VMEM((ROW_TILE, d), BF16)],
        compiler_params=_params("parallel"),
        name="norm_matmul",
    )(x, g.reshape(1, d), w)


def _sb_scores(qm, kb, mask):
    z = _dot_nt(qm, kb)
    e = jnp.exp2(jnp.abs(z) * (-LOG2E))
    nlk = jnp.log(1.0 + e) + jnp.maximum(z, 0.0)
    if mask is not None:
        nlk = jnp.where(mask, nlk, 0.0)
    hi = nlk.astype(BF16)
    lo = (nlk - hi.astype(F32)).astype(BF16)
    return z, jnp.concatenate([hi, lo], axis=1)


def _sb_weighted_values(z, hilo, vb, tri2, carry, mask):
    neg_suffix = _dot(hilo, tri2)
    a = jnp.exp2((neg_suffix + (z - carry)) * LOG2E)
    if mask is not None:
        a = jnp.where(mask, a, 0.0)
    return _dot(a.astype(BF16), vb), carry - neg_suffix[:, 0:1]


def _sb_attention_kernel(q_ref, k_ref, v_ref, o_ref, qs_ref, acc_ref, car_ref):
    seq = q_ref.shape[0]
    blk = SB_BLOCK
    sup = 2 * blk
    first_head = lax.broadcasted_iota(jnp.int32, (blk, LANES), 1) < SB_HEAD_DIM
    r = lax.broadcasted_iota(jnp.int32, (2 * blk, blk), 0) & (blk - 1)
    c = lax.broadcasted_iota(jnp.int32, (2 * blk, blk), 1)
    tri2 = jnp.where(r >= c, -1.0, 0.0).astype(BF16)
    diag_mask = c < r
    scale = jnp.asarray(SB_HEAD_DIM ** -0.5, BF16)

    def stack_heads(q):
        zero = jnp.zeros_like(q)
        return jnp.concatenate([jnp.where(first_head, q, zero), jnp.where(first_head, zero, q)], axis=0)

    def merge_heads(acc):
        return jnp.where(first_head, acc[:blk], acc[blk:]).astype(o_ref.dtype)

    def super_block(m, _):
        r0 = m * sup
        lo_rows, hi_rows = pl.ds(r0, blk), pl.ds(r0 + blk, blk)
        q_lo = stack_heads(q_ref[lo_rows, :] * scale)
        q_hi = stack_heads(q_ref[hi_rows, :] * scale)
        qs_ref[...] = jnp.concatenate([q_lo, q_hi], axis=0)
        z_ll, s_ll = _sb_scores(q_lo, k_ref[lo_rows, :], diag_mask)
        z_hh, s_hh = _sb_scores(q_hi, k_ref[hi_rows, :], diag_mask)
        z_hl, s_hl = _sb_scores(q_hi, k_ref[lo_rows, :], None)
        no_carry = jnp.zeros((2 * blk, 1), F32)
        o_ll, c_lo = _sb_weighted_values(z_ll, s_ll, v_ref[lo_rows, :], tri2, no_carry, diag_mask)
        o_hh, c_hi = _sb_weighted_values(z_hh, s_hh, v_ref[hi_rows, :], tri2, no_carry, diag_mask)
        o_hl, c_hi = _sb_weighted_values(z_hl, s_hl, v_ref[lo_rows, :], tri2, c_hi, None)
        acc_ref[...] = jnp.concatenate([o_ll, o_hh + o_hl], axis=0)
        car_ref[...] = jnp.concatenate([c_lo, c_hi], axis=0)

        def key_super(jj, _):
            s0 = (m - 1 - jj) * sup
            right, left = pl.ds(s0 + blk, blk), pl.ds(s0, blk)
            qs = qs_ref[...]
            z_r, s_r = _sb_scores(qs, k_ref[right, :], None)
            z_l, s_l = _sb_scores(qs, k_ref[left, :], None)
            o_r, car = _sb_weighted_values(z_r, s_r, v_ref[right, :], tri2, car_ref[...], None)
            o_l, car = _sb_weighted_values(z_l, s_l, v_ref[left, :], tri2, car, None)
            car_ref[...] = car
            acc_ref[...] += o_r + o_l
            return 0

        for jj in range(m):
            key_super(jj, 0)
        o_ref[lo_rows, :] = merge_heads(acc_ref[0:2 * blk, :])
        o_ref[hi_rows, :] = merge_heads(acc_ref[2 * blk:4 * blk, :])
        return 0

    for m in range(seq // sup):
        super_block(m, 0)


def _sb_attention(qkv):
    b, s, _ = qkv.shape
    col_blocks = D_MODEL // LANES
    chains = 4 * SB_BLOCK
    return pl.pallas_call(
        _sb_attention_kernel,
        grid=(b, col_blocks),
        in_specs=[
            pl.BlockSpec((None, s, LANES), lambda bi, c: (bi, 0, c)),
            pl.BlockSpec((None, s, LANES), lambda bi, c: (bi, 0, col_blocks + c)),
            pl.BlockSpec((None, s, LANES), lambda bi, c: (bi, 0, 2 * col_blocks + c)),
        ],
        out_specs=pl.BlockSpec((None, s, LANES), lambda bi, c: (bi, 0, c)),
        out_shape=jax.ShapeDtypeStruct((b, s, D_MODEL), BF16),
        scratch_shapes=[
            pltpu.VMEM((chains, LANES), BF16),
            pltpu.VMEM((chains, LANES), F32),
            pltpu.VMEM((chains, 1), F32),
        ],
        compiler_params=_params("parallel", "parallel"),
        name="sb_attention",
    )(qkv, qkv, qkv)


def _rotate(x, cos, sin):
    half = x.shape[-1] // 2
    x1, x2 = x[:, :half], x[:, half:]
    return jnp.concatenate([x1 * cos - x2 * sin, x2 * cos + x1 * sin], axis=-1)


def _retention_kernel(dchunk_ref, proj_ref, cos_ref, sin_ref, dintra_ref, dq_ref, dk_ref, o_ref, state_ref):
    @pl.when(pl.program_id(1) == 0)
    def _():
        state_ref[...] = jnp.zeros_like(state_ref)

    for n in range(proj_ref.shape[0] // BLOCK):
        rows = slice(n * BLOCK, (n + 1) * BLOCK)
        cos = cos_ref[rows, :]
        sin = sin_ref[rows, :]
        for h in range(RET_HEADS):
            q_cols = slice(h * RET_QK_DIM, (h + 1) * RET_QK_DIM)
            k_cols = slice(D_MODEL + h * RET_QK_DIM, D_MODEL + (h + 1) * RET_QK_DIM)
            v_cols = slice(2 * D_MODEL + h * RET_V_DIM, 2 * D_MODEL + (h + 1) * RET_V_DIM)
            g_cols = slice(4 * D_MODEL + h * RET_V_DIM, 4 * D_MODEL + (h + 1) * RET_V_DIM)
            q = _rotate(proj_ref[rows, q_cols].astype(F32), cos, sin)
            k = _rotate(proj_ref[rows, k_cols].astype(F32), cos, sin) * (RET_QK_DIM ** -0.5)
            v = proj_ref[rows, v_cols]
            qb = q.astype(BF16)
            scores = _dot_nt(qb, k.astype(BF16)) * dintra_ref[h]
            intra = _dot(scores.astype(BF16), v)
            state = state_ref[h]
            inter = _dot(qb, state.astype(BF16))
            decay_k = jnp.concatenate([dk_ref[h]] * (RET_QK_DIM // LANES), axis=1)
            k_dec_t = (k * decay_k).T.astype(BF16)
            state_ref[h] = state * dchunk_ref[h] + _dot(k_dec_t, v)
            decay_q = jnp.concatenate([dq_ref[h]] * (RET_V_DIM // LANES), axis=1)
            o = intra + inter * decay_q
            mu = jnp.mean(o, axis=-1, keepdims=True)
            oc = o - mu
            var = jnp.mean(oc * oc, axis=-1, keepdims=True)
            on = oc * lax.rsqrt(var + EPS)
            gate = proj_ref[rows, g_cols].astype(F32)
            o_ref[rows, h * RET_V_DIM:(h + 1) * RET_V_DIM] = (gate * jax.nn.sigmoid(gate) * on).astype(o_ref.dtype)


def _retention_tables(seq):
    half = RET_QK_DIM // 2
    inv_freq = 1.0 / (ROPE_BASE ** (jnp.arange(half, dtype=F32) / half))
    ang = jnp.arange(seq, dtype=F32)[:, None] * inv_freq[None, :]
    log_gamma = jnp.log1p(-jnp.exp2(-5.0 - jnp.arange(RET_HEADS, dtype=F32)))
    idx = jnp.arange(BLOCK, dtype=F32)
    rel = idx[:, None] - idx[None, :]
    lg = log_gamma[:, None, None]
    d_intra = jnp.where(rel >= 0, jnp.exp(lg * jnp.maximum(rel, 0.0)), 0.0)
    rows = jnp.broadcast_to(idx[None, :, None], (RET_HEADS, BLOCK, LANES))
    d_q = jnp.exp(lg * (rows + 1.0))
    d_k = jnp.exp(lg * (BLOCK - 1.0 - rows))
    d_chunk = jnp.exp(log_gamma * BLOCK)
    return jnp.cos(ang), jnp.sin(ang), d_intra, d_q, d_k, d_chunk


def _retention(proj, tables):
    b, s, width = proj.shape
    cos, sin, d_intra, d_q, d_k, d_chunk = tables
    half = RET_QK_DIM // 2
    table = pl.BlockSpec((RET_HEADS, BLOCK, LANES), lambda bi, n: (0, 0, 0))
    return pl.pallas_call(
        _retention_kernel,
        grid=(b, s // RET_ROWS),
        in_specs=[
            pl.BlockSpec(memory_space=pltpu.SMEM),
            pl.BlockSpec((None, RET_ROWS, width), lambda bi, n: (bi, n, 0)),
            pl.BlockSpec((RET_ROWS, half), lambda bi, n: (n, 0)),
            pl.BlockSpec((RET_ROWS, half), lambda bi, n: (n, 0)),
            table, table, table,
        ],
        out_specs=pl.BlockSpec((None, RET_ROWS, RET_HEADS * RET_V_DIM), lambda bi, n: (bi, n, 0)),
        out_shape=jax.ShapeDtypeStruct((b, s, RET_HEADS * RET_V_DIM), BF16),
        scratch_shapes=[pltpu.VMEM((RET_HEADS, RET_QK_DIM, RET_V_DIM), F32)],
        compiler_params=_params("parallel", "arbitrary"),
        name="retention",
    )(d_chunk, proj, cos, sin, d_intra, d_q, d_k)


def _sgu_kernel(z_ref, x_ref, nrm_ref, ws_ref, bs_ref, wo_ref, o_ref, vn_ref, um_ref):
    rows = z_ref.shape[0]
    v = z_ref[:, SGU_WIDTH:].astype(F32)
    vn_ref[...] = _rmsnorm(v, nrm_ref[...]).astype(BF16)
    r = lax.broadcasted_iota(jnp.int32, (BLOCK, BLOCK), 0)
    c = lax.broadcasted_iota(jnp.int32, (BLOCK, BLOCK), 1)
    causal = c <= r
    for g in range(SGU_GROUPS):
        w = jnp.where(causal, ws_ref[g], 0.0).astype(BF16)
        bias = bs_ref[g]
        cols = slice(g * SGU_GROUP_DIM, (g + 1) * SGU_GROUP_DIM)
        for n in range(rows // BLOCK):
            rs = slice(n * BLOCK, (n + 1) * BLOCK)
            mixed = _dot(w, vn_ref[rs, cols]) + bias
            um_ref[rs, cols] = (z_ref[rs, cols].astype(F32) * mixed).astype(BF16)
    o_ref[...] = x_ref[...] + _dot(um_ref[...], wo_ref[...])


def _sgu(z, x, sgu_norm, w_s, b_s, w_out, *, row_tile=ROW_TILE):
    rows, d = x.shape
    return pl.pallas_call(
        _sgu_kernel,
        grid=(rows // row_tile,),
        in_specs=[
            pl.BlockSpec((row_tile, 2 * SGU_WIDTH), lambda i: (i, 0)),
            pl.BlockSpec((row_tile, d), lambda i: (i, 0)),
            pl.BlockSpec((1, SGU_WIDTH), lambda i: (0, 0)),
            pl.BlockSpec((SGU_GROUPS, BLOCK, BLOCK), lambda i: (0, 0, 0)),
            pl.BlockSpec((SGU_GROUPS, BLOCK, 1), lambda i: (0, 0, 0)),
            pl.BlockSpec((SGU_WIDTH, d), lambda i: (0, 0)),
        ],
        out_specs=pl.BlockSpec((row_tile, d), lambda i: (i, 0)),
        out_shape=jax.ShapeDtypeStruct((rows, d), F32),
        scratch_shapes=[pltpu.VMEM((row_tile, SGU_WIDTH), BF16), pltpu.VMEM((row_tile, SGU_WIDTH), BF16)],
        compiler_params=_params("parallel"),
        name="sgu",
    )(z, x, sgu_norm.reshape(1, SGU_WIDTH), w_s, b_s.reshape(SGU_GROUPS, BLOCK, 1), w_out)


def _layer_tail_kernel(*refs, has_mix, final, tiles_per_seq):
    if has_mix:
        a_ref, wmix_ref, *refs = refs
    (x_ref, gffn_ref, win_ref, cw_ref, cb_ref, wout_ref, p_ref, gpl_ref, wgate_ref, wproj_ref, gfin_ref,
     o_ref, res_ref, h_ref, act_ref, carry_ref) = refs
    rows = x_ref.shape[0]
    halo = SUBLANES

    x1 = x_ref[...]
    if has_mix:
        x1 = x1 + _dot(a_ref[...], wmix_ref[...])
    res_ref[...] = x1

    h_ref[...] = _rmsnorm(x1, gffn_ref[...]).astype(BF16)
    seq_start = (pl.program_id(0) % tiles_per_seq) == 0
    for c in range(FFN_DIM // FFN_CHUNK):
        cols = slice(c * FFN_CHUNK, (c + 1) * FFN_CHUNK)
        up_cols = slice(FFN_DIM + c * FFN_CHUNK, FFN_DIM + (c + 1) * FFN_CHUNK)
        h = h_ref[...]
        gate = _dot(h, win_ref[:, cols])
        up = _dot(h, win_ref[:, up_cols])
        prev = jnp.where(seq_start, 0.0, carry_ref[:, cols])
        carry_ref[:, cols] = gate[rows - halo:, :]
        ext = jnp.concatenate([prev, gate], axis=0)
        g1 = pltpu.roll(ext, 1, axis=0)[halo:]
        g2 = pltpu.roll(ext, 2, axis=0)[halo:]
        conv = cb_ref[:, cols] + cw_ref[0:1, cols] * g2 + cw_ref[1:2, cols] * g1 + cw_ref[2:3, cols] * gate
        act_ref[:, cols] = (conv * jax.nn.sigmoid(conv) * up).astype(BF16)
    x2 = res_ref[...] + _dot(act_ref[...], wout_ref[...])
    res_ref[...] = x2

    h_ref[...] = _rmsnorm(x2, gpl_ref[...]).astype(BF16)
    gate = jax.nn.sigmoid(_dot(h_ref[...], wgate_ref[...]))
    y = res_ref[...] + gate * _dot(p_ref[...].astype(BF16), wproj_ref[...])
    if final:
        y = _rmsnorm(y, gfin_ref[...])
    o_ref[...] = y


def _layer_tail(x, mix, w_mix, g_ffn, w_in, conv_w, conv_b, w_out, p, layer, g_pl, w_gate, w_proj, g_final,
                *, seq, final):
    rows, d = x.shape
    has_mix = mix is not None
    row_block = lambda width: pl.BlockSpec((ROW_TILE, width), lambda i: (i, 0))
    in_specs, args = [], []
    if has_mix:
        k = mix.shape[1]
        in_specs += [row_block(k), _resident((k, d))]
        args += [mix, w_mix]
    in_specs += [
        row_block(d),
        _resident((1, d)),
        _resident((d, 2 * FFN_DIM)),
        _resident((CONV_WIDTH, FFN_DIM)),
        _resident((1, FFN_DIM)),
        _resident((FFN_DIM, d)),
        pl.BlockSpec((None, ROW_TILE, PL_DIM), lambda i: (layer, i, 0)),
        _resident((1, d)),
        _resident((d, d)),
        _resident((PL_DIM, d)),
        _resident((1, d)),
    ]
    args += [x, g_ffn.reshape(1, d), w_in, conv_w, conv_b.reshape(1, FFN_DIM), w_out,
             p, g_pl.reshape(1, d), w_gate, w_proj, g_final.reshape(1, d)]
    return pl.pallas_call(
        functools.partial(_layer_tail_kernel, has_mix=has_mix, final=final, tiles_per_seq=seq // ROW_TILE),
        grid=(rows // ROW_TILE,),
        in_specs=in_specs,
        out_specs=row_block(d),
        out_shape=jax.ShapeDtypeStruct((rows, d), F32),
        scratch_shapes=[
            pltpu.VMEM((ROW_TILE, d), F32),
            pltpu.VMEM((ROW_TILE, d), BF16),
            pltpu.VMEM((ROW_TILE, FFN_DIM), BF16),
            pltpu.VMEM((SUBLANES, FFN_DIM), F32),
        ],
        compiler_params=_params("arbitrary"),
        name="layer_tail",
    )(*args)


def kernel(x, p, norm_mix, norm_ffn, norm_pl, norm_final, sb_w_in, sb_w_out, ret_w_in, ret_w_out, sgu_w_in, sgu_norm, sgu_w_s, sgu_b_s, sgu_w_out, ffn_w_in, ffn_conv_w, ffn_conv_b, ffn_w_out, pl_w_gate, pl_w_proj):
    b, s, d = x.shape
    rows = b * s
    xf = x.reshape(rows, d)
    pf = p.reshape(DEPTH, rows, PL_DIM)
    ret_tables = _retention_tables(s)
    for i in range(DEPTH):
        kind = i % N_MIXERS
        j = i // N_MIXERS
        if kind == 0:
            qkv = _norm_matmul(xf, norm_mix[i], sb_w_in[j].astype(BF16))
            mix = _sb_attention(qkv.reshape(b, s, 3 * d)).reshape(rows, d)
            w_mix = sb_w_out[j].astype(BF16)
        elif kind == 1:
            proj = _norm_matmul(xf, norm_mix[i], ret_w_in[j].astype(BF16))
            mix = _retention(proj.reshape(b, s, 6 * d), ret_tables).reshape(rows, 2 * d)
            w_mix = ret_w_out[j].astype(BF16)
        else:
            z = _norm_matmul(xf, norm_mix[i], sgu_w_in[j].astype(BF16), gelu=True)
            xf = _sgu(z, xf, sgu_norm[j], sgu_w_s[j], sgu_b_s[j], sgu_w_out[j].astype(BF16))
            mix = w_mix = None
        xf = _layer_tail(xf, mix, w_mix, norm_ffn[i], ffn_w_in[i].astype(BF16), ffn_conv_w[i], ffn_conv_b[i],
                         ffn_w_out[i].astype(BF16), pf, i, norm_pl[i], pl_w_gate[i].astype(BF16),
                         pl_w_proj[i].astype(BF16), norm_final, seq=s, final=(i == DEPTH - 1))
    return xf.reshape(b, s, d)
```

```python
import functools

import jax
import jax.numpy as jnp
from jax import lax
from jax.experimental import pallas as pl
from jax.experimental.pallas import tpu as pltpu

D_MODEL = 1024
DEPTH = 4
N_MIXERS = 3
PL_DIM = 256
EPS = 1e-6
BLOCK = 128

SB_HEADS = 16
SB_HEAD_DIM = D_MODEL // SB_HEADS
RET_HEADS = 4
RET_QK_DIM = D_MODEL // RET_HEADS
RET_V_DIM = 2 * D_MODEL // RET_HEADS
ROPE_BASE = 10000.0
SGU_WIDTH = 2 * D_MODEL
SGU_GROUPS = 8
SGU_GROUP_DIM = SGU_WIDTH // SGU_GROUPS
FFN_DIM = 2816
CONV_WIDTH = 3

LANES = 128
SUBLANES = 8
MXU_DIM = 256
VMEM_LIMIT_BYTES = 56 * 1024 * 1024

ROW_TILE = 512
PROJ_CHUNK = 512
FFN_CHUNK = MXU_DIM
RET_ROWS = 2 * BLOCK
SB_BLOCK = MXU_DIM
LOG2E = 1.4426950408889634

F32 = jnp.float32
BF16 = jnp.bfloat16


def _params(*semantics):
    return pltpu.CompilerParams(dimension_semantics=semantics, vmem_limit_bytes=VMEM_LIMIT_BYTES)


def _rmsnorm(x, g):
    return x * lax.rsqrt(jnp.mean(x * x, axis=-1, keepdims=True) + EPS) * g


def _dot(a, b):
    return jnp.dot(a, b, preferred_element_type=F32)


def _dot_nt(a, b):
    return lax.dot_general(a, b, (((1,), (1,)), ((), ())), preferred_element_type=F32)


def _resident(shape):
    return pl.BlockSpec(shape, lambda i: (0,) * len(shape), pipeline_mode=pl.Buffered(1))


def _norm_matmul_kernel(x_ref, g_ref, w_ref, o_ref, h_ref, *, gelu):
    h_ref[...] = _rmsnorm(x_ref[...], g_ref[...]).astype(BF16)
    for c in range(w_ref.shape[1] // PROJ_CHUNK):
        cols = slice(c * PROJ_CHUNK, (c + 1) * PROJ_CHUNK)
        acc = _dot(h_ref[...], w_ref[:, cols])
        if gelu:
            acc = jax.nn.gelu(acc)
        o_ref[:, cols] = acc.astype(o_ref.dtype)


def _norm_matmul(x, g, w, *, gelu=False):
    rows, d = x.shape
    n = w.shape[1]
    return pl.pallas_call(
        functools.partial(_norm_matmul_kernel, gelu=gelu),
        grid=(rows // ROW_TILE,),
        in_specs=[
            pl.BlockSpec((ROW_TILE, d), lambda i: (i, 0)),
            _resident((1, d)),
            _resident((d, n)),
        ],
        out_specs=pl.BlockSpec((ROW_TILE, n), lambda i: (i, 0)),
        out_shape=jax.ShapeDtypeStruct((rows, n), BF16),
        scratch_shapes=[pltpu.VMEM((ROW_TILE, d), BF16)],
        compiler_params=_params("parallel"),
        name="norm_matmul",
    )(x, g.reshape(1, d), w)


def _sb_scores(qm, kb, mask):
    z = _dot_nt(qm, kb)
    e = jnp.exp2(jnp.abs(z) * (-LOG2E))
    nlk = jnp.log(1.0 + e) + jnp.maximum(z, 0.0)
    if mask is not None:
        nlk = jnp.where(mask, nlk, 0.0)
    hi = nlk.astype(BF16)
    lo = (nlk - hi.astype(F32)).astype(BF16)
    return z, jnp.concatenate([hi, lo], axis=1)


def _sb_weighted_values(z, hilo, vb, tri2, carry, mask):
    neg_suffix = _dot(hilo, tri2)
    a = jnp.exp2((neg_suffix + (z - carry)) * LOG2E)
    if mask is not None:
        a = jnp.where(mask, a, 0.0)
    return _dot(a.astype(BF16), vb), carry - neg_suffix[:, 0:1]


def _sb_attention_kernel(q_ref, k_ref, v_ref, o_ref, qs_ref, acc_ref, car_ref):
    seq = q_ref.shape[0]
    blk = SB_BLOCK
    sup = 2 * blk
    first_head = lax.broadcasted_iota(jnp.int32, (blk, LANES), 1) < SB_HEAD_DIM
    r = lax.broadcasted_iota(jnp.int32, (2 * blk, blk), 0) & (blk - 1)
    c = lax.broadcasted_iota(jnp.int32, (2 * blk, blk), 1)
    tri2 = jnp.where(r >= c, -1.0, 0.0).astype(BF16)
    diag_mask = c < r
    scale = jnp.asarray(SB_HEAD_DIM ** -0.5, BF16)

    def stack_heads(q):
        zero = jnp.zeros_like(q)
        return jnp.concatenate([jnp.where(first_head, q, zero), jnp.where(first_head, zero, q)], axis=0)

    def merge_heads(acc):
        return jnp.where(first_head, acc[:blk], acc[blk:]).astype(o_ref.dtype)

    def store_output(m):
        o_ref[pl.ds(m * sup, blk), :] = merge_heads(acc_ref[0:2 * blk, :])
        o_ref[pl.ds(m * sup + blk, blk), :] = merge_heads(acc_ref[2 * blk:4 * blk, :])

    def diag_group(m):
        lo_rows, hi_rows = pl.ds(m * sup, blk), pl.ds(m * sup + blk, blk)

        def scores():
            q_lo = stack_heads(q_ref[lo_rows, :] * scale)
            q_hi = stack_heads(q_ref[hi_rows, :] * scale)
            qs_ref[m % 2] = jnp.concatenate([q_lo, q_hi], axis=0)
            return (_sb_scores(q_lo, k_ref[lo_rows, :], diag_mask),
                    _sb_scores(q_hi, k_ref[hi_rows, :], diag_mask),
                    _sb_scores(q_hi, k_ref[lo_rows, :], None))

        def values(s):
            (z_ll, s_ll), (z_hh, s_hh), (z_hl, s_hl) = s
            no_carry = jnp.zeros((2 * blk, 1), F32)
            o_ll, c_lo = _sb_weighted_values(z_ll, s_ll, v_ref[lo_rows, :], tri2, no_carry, diag_mask)
            o_hh, c_hi = _sb_weighted_values(z_hh, s_hh, v_ref[hi_rows, :], tri2, no_carry, diag_mask)
            o_hl, c_hi = _sb_weighted_values(z_hl, s_hl, v_ref[lo_rows, :], tri2, c_hi, None)
            acc_ref[...] = jnp.concatenate([o_ll, o_hh + o_hl], axis=0)
            car_ref[...] = jnp.concatenate([c_lo, c_hi], axis=0)
            if m == 0:
                store_output(m)

        return scores, values

    def full_group(m, j):
        right, left = pl.ds(j * sup + blk, blk), pl.ds(j * sup, blk)

        def scores():
            qs = qs_ref[m % 2]
            return _sb_scores(qs, k_ref[right, :], None), _sb_scores(qs, k_ref[left, :], None)

        def values(s):
            (z_r, s_r), (z_l, s_l) = s
            o_r, car = _sb_weighted_values(z_r, s_r, v_ref[right, :], tri2, car_ref[...], None)
            o_l, car = _sb_weighted_values(z_l, s_l, v_ref[left, :], tri2, car, None)
            car_ref[...] = car
            acc_ref[...] += o_r + o_l
            if j == 0:
                store_output(m)

        return scores, values

    groups = []
    for m in range(seq // sup):
        groups.append(diag_group(m))
        groups += [full_group(m, j) for j in range(m - 1, -1, -1)]
    pending = groups[0][0]()
    for g, (_, values) in enumerate(groups):
        following = groups[g + 1][0]() if g + 1 < len(groups) else None
        values(pending)
        pending = following


def _sb_attention(qkv):
    b, s, _ = qkv.shape
    col_blocks = D_MODEL // LANES
    chains = 4 * SB_BLOCK
    return pl.pallas_call(
        _sb_attention_kernel,
        grid=(b, col_blocks),
        in_specs=[
            pl.BlockSpec((None, s, LANES), lambda bi, c: (bi, 0, c)),
            pl.BlockSpec((None, s, LANES), lambda bi, c: (bi, 0, col_blocks + c)),
            pl.BlockSpec((None, s, LANES), lambda bi, c: (bi, 0, 2 * col_blocks + c)),
        ],
        out_specs=pl.BlockSpec((None, s, LANES), lambda bi, c: (bi, 0, c)),
        out_shape=jax.ShapeDtypeStruct((b, s, D_MODEL), BF16),
        scratch_shapes=[
            pltpu.VMEM((2, chains, LANES), BF16),
            pltpu.VMEM((chains, LANES), F32),
            pltpu.VMEM((chains, 1), F32),
        ],
        compiler_params=_params("parallel", "parallel"),
        name="sb_attention",
    )(qkv, qkv, qkv)


def _rotate(x, cos, sin):
    half = x.shape[-1] // 2
    x1, x2 = x[:, :half], x[:, half:]
    return jnp.concatenate([x1 * cos - x2 * sin, x2 * cos + x1 * sin], axis=-1)


def _retention_kernel(dchunk_ref, proj_ref, cos_ref, sin_ref, dintra_ref, dq_ref, dk_ref, o_ref, state_ref):
    @pl.when(pl.program_id(1) == 0)
    def _():
        state_ref[...] = jnp.zeros_like(state_ref)

    for n in range(proj_ref.shape[0] // BLOCK):
        rows = slice(n * BLOCK, (n + 1) * BLOCK)
        cos = cos_ref[rows, :]
        sin = sin_ref[rows, :]
        for h in range(RET_HEADS):
            q_cols = slice(h * RET_QK_DIM, (h + 1) * RET_QK_DIM)
            k_cols = slice(D_MODEL + h * RET_QK_DIM, D_MODEL + (h + 1) * RET_QK_DIM)
            v_cols = slice(2 * D_MODEL + h * RET_V_DIM, 2 * D_MODEL + (h + 1) * RET_V_DIM)
            g_cols = slice(4 * D_MODEL + h * RET_V_DIM, 4 * D_MODEL + (h + 1) * RET_V_DIM)
            q = _rotate(proj_ref[rows, q_cols].astype(F32), cos, sin)
            k = _rotate(proj_ref[rows, k_cols].astype(F32), cos, sin) * (RET_QK_DIM ** -0.5)
            v = proj_ref[rows, v_cols]
            qb = q.astype(BF16)
            scores = _dot_nt(qb, k.astype(BF16)) * dintra_ref[h]
            intra = _dot(scores.astype(BF16), v)
            state = state_ref[h]
            inter = _dot(qb, state.astype(BF16))
            decay_k = jnp.concatenate([dk_ref[h]] * (RET_QK_DIM // LANES), axis=1)
            k_dec_t = (k * decay_k).T.astype(BF16)
            state_ref[h] = state * dchunk_ref[h] + _dot(k_dec_t, v)
            decay_q = jnp.concatenate([dq_ref[h]] * (RET_V_DIM // LANES), axis=1)
            o = intra + inter * decay_q
            mu = jnp.mean(o, axis=-1, keepdims=True)
            oc = o - mu
            var = jnp.mean(oc * oc, axis=-1, keepdims=True)
            on = oc * lax.rsqrt(var + EPS)
            gate = proj_ref[rows, g_cols].astype(F32)
            o_ref[rows, h * RET_V_DIM:(h + 1) * RET_V_DIM] = (gate * jax.nn.sigmoid(gate) * on).astype(o_ref.dtype)


def _retention_tables(seq):
    half = RET_QK_DIM // 2
    inv_freq = 1.0 / (ROPE_BASE ** (jnp.arange(half, dtype=F32) / half))
    ang = jnp.arange(seq, dtype=F32)[:, None] * inv_freq[None, :]
    log_gamma = jnp.log1p(-jnp.exp2(-5.0 - jnp.arange(RET_HEADS, dtype=F32)))
    idx = jnp.arange(BLOCK, dtype=F32)
    rel = idx[:, None] - idx[None, :]
    lg = log_gamma[:, None, None]
    d_intra = jnp.where(rel >= 0, jnp.exp(lg * jnp.maximum(rel, 0.0)), 0.0)
    rows = jnp.broadcast_to(idx[None, :, None], (RET_HEADS, BLOCK, LANES))
    d_q = jnp.exp(lg * (rows + 1.0))
    d_k = jnp.exp(lg * (BLOCK - 1.0 - rows))
    d_chunk = jnp.exp(log_gamma * BLOCK)
    return jnp.cos(ang), jnp.sin(ang), d_intra, d_q, d_k, d_chunk


def _retention(proj, tables):
    b, s, width = proj.shape
    cos, sin, d_intra, d_q, d_k, d_chunk = tables
    half = RET_QK_DIM // 2
    table = pl.BlockSpec((RET_HEADS, BLOCK, LANES), lambda bi, n: (0, 0, 0))
    return pl.pallas_call(
        _retention_kernel,
        grid=(b, s // RET_ROWS),
        in_specs=[
            pl.BlockSpec(memory_space=pltpu.SMEM),
            pl.BlockSpec((None, RET_ROWS, width), lambda bi, n: (bi, n, 0)),
            pl.BlockSpec((RET_ROWS, half), lambda bi, n: (n, 0)),
            pl.BlockSpec((RET_ROWS, half), lambda bi, n: (n, 0)),
            table, table, table,
        ],
        out_specs=pl.BlockSpec((None, RET_ROWS, RET_HEADS * RET_V_DIM), lambda bi, n: (bi, n, 0)),
        out_shape=jax.ShapeDtypeStruct((b, s, RET_HEADS * RET_V_DIM), BF16),
        scratch_shapes=[pltpu.VMEM((RET_HEADS, RET_QK_DIM, RET_V_DIM), F32)],
        compiler_params=_params("parallel", "arbitrary"),
        name="retention",
    )(d_chunk, proj, cos, sin, d_intra, d_q, d_k)


def _sgu_kernel(z_ref, x_ref, nrm_ref, ws_ref, bs_ref, wo_ref, o_ref, vn_ref, um_ref):
    rows = z_ref.shape[0]
    v = z_ref[:, SGU_WIDTH:].astype(F32)
    vn_ref[...] = _rmsnorm(v, nrm_ref[...]).astype(BF16)
    r = lax.broadcasted_iota(jnp.int32, (BLOCK, BLOCK), 0)
    c = lax.broadcasted_iota(jnp.int32, (BLOCK, BLOCK), 1)
    causal = c <= r
    for g in range(SGU_GROUPS):
        w = jnp.where(causal, ws_ref[g], 0.0).astype(BF16)
        bias = bs_ref[g]
        cols = slice(g * SGU_GROUP_DIM, (g + 1) * SGU_GROUP_DIM)
        for n in range(rows // BLOCK):
            rs = slice(n * BLOCK, (n + 1) * BLOCK)
            mixed = _dot(w, vn_ref[rs, cols]) + bias
            um_ref[rs, cols] = (z_ref[rs, cols].astype(F32) * mixed).astype(BF16)
    o_ref[...] = x_ref[...] + _dot(um_ref[...], wo_ref[...])


def _sgu(z, x, sgu_norm, w_s, b_s, w_out, *, row_tile=ROW_TILE):
    rows, d = x.shape
    return pl.pallas_call(
        _sgu_kernel,
        grid=(rows // row_tile,),
        in_specs=[
            pl.BlockSpec((row_tile, 2 * SGU_WIDTH), lambda i: (i, 0)),
            pl.BlockSpec((row_tile, d), lambda i: (i, 0)),
            pl.BlockSpec((1, SGU_WIDTH), lambda i: (0, 0)),
            pl.BlockSpec((SGU_GROUPS, BLOCK, BLOCK), lambda i: (0, 0, 0)),
            pl.BlockSpec((SGU_GROUPS, BLOCK, 1), lambda i: (0, 0, 0)),
            pl.BlockSpec((SGU_WIDTH, d), lambda i: (0, 0)),
        ],
        out_specs=pl.BlockSpec((row_tile, d), lambda i: (i, 0)),
        out_shape=jax.ShapeDtypeStruct((rows, d), F32),
        scratch_shapes=[pltpu.VMEM((row_tile, SGU_WIDTH), BF16), pltpu.VMEM((row_tile, SGU_WIDTH), BF16)],
        compiler_params=_params("parallel"),
        name="sgu",
    )(z, x, sgu_norm.reshape(1, SGU_WIDTH), w_s, b_s.reshape(SGU_GROUPS, BLOCK, 1), w_out)


def _layer_tail_kernel(*refs, has_mix, final, tiles_per_seq):
    if has_mix:
        a_ref, wmix_ref, *refs = refs
    (x_ref, gffn_ref, win_ref, cw_ref, cb_ref, wout_ref, p_ref, gpl_ref, wgate_ref, wproj_ref, gfin_ref,
     o_ref, res_ref, h_ref, act_ref, carry_ref) = refs
    rows = x_ref.shape[0]
    halo = SUBLANES

    x1 = x_ref[...]
    if has_mix:
        x1 = x1 + _dot(a_ref[...], wmix_ref[...])
    res_ref[...] = x1

    h_ref[...] = _rmsnorm(x1, gffn_ref[...]).astype(BF16)
    seq_start = (pl.program_id(0) % tiles_per_seq) == 0
    for c in range(FFN_DIM // FFN_CHUNK):
        cols = slice(c * FFN_CHUNK, (c + 1) * FFN_CHUNK)
        up_cols = slice(FFN_DIM + c * FFN_CHUNK, FFN_DIM + (c + 1) * FFN_CHUNK)
        h = h_ref[...]
        gate = _dot(h, win_ref[:, cols])
        up = _dot(h, win_ref[:, up_cols])
        prev = jnp.where(seq_start, 0.0, carry_ref[:, cols])
        carry_ref[:, cols] = gate[rows - halo:, :]
        ext = jnp.concatenate([prev, gate], axis=0)
        g1 = pltpu.roll(ext, 1, axis=0)[halo:]
        g2 = pltpu.roll(ext, 2, axis=0)[halo:]
        conv = cb_ref[:, cols] + cw_ref[0:1, cols] * g2 + cw_ref[1:2, cols] * g1 + cw_ref[2:3, cols] * gate
        act_ref[:, cols] = (conv * jax.nn.sigmoid(conv) * up).astype(BF16)
    x2 = res_ref[...] + _dot(act_ref[...], wout_ref[...])
    res_ref[...] = x2

    h_ref[...] = _rmsnorm(x2, gpl_ref[...]).astype(BF16)
    gate = jax.nn.sigmoid(_dot(h_ref[...], wgate_ref[...]))
    y = res_ref[...] + gate * _dot(p_ref[...].astype(BF16), wproj_ref[...])
    if final:
        y = _rmsnorm(y, gfin_ref[...])
    o_ref[...] = y


def _layer_tail(x, mix, w_mix, g_ffn, w_in, conv_w, conv_b, w_out, p, layer, g_pl, w_gate, w_proj, g_final,
                *, seq, final):
    rows, d = x.shape
    has_mix = mix is not None
    row_block = lambda width: pl.BlockSpec((ROW_TILE, width), lambda i: (i, 0))
    in_specs, args = [], []
    if has_mix:
        k = mix.shape[1]
        in_specs += [row_block(k), _resident((k, d))]
        args += [mix, w_mix]
    in_specs += [
        row_block(d),
        _resident((1, d)),
        _resident((d, 2 * FFN_DIM)),
        _resident((CONV_WIDTH, FFN_DIM)),
        _resident((1, FFN_DIM)),
        _resident((FFN_DIM, d)),
        pl.BlockSpec((None, ROW_TILE, PL_DIM), lambda i: (layer, i, 0)),
        _resident((1, d)),
        _resident((d, d)),
        _resident((PL_DIM, d)),
        _resident((1, d)),
    ]
    args += [x, g_ffn.reshape(1, d), w_in, conv_w, conv_b.reshape(1, FFN_DIM), w_out,
             p, g_pl.reshape(1, d), w_gate, w_proj, g_final.reshape(1, d)]
    return pl.pallas_call(
        functools.partial(_layer_tail_kernel, has_mix=has_mix, final=final, tiles_per_seq=seq // ROW_TILE),
        grid=(rows // ROW_TILE,),
        in_specs=in_specs,
        out_specs=row_block(d),
        out_shape=jax.ShapeDtypeStruct((rows, d), F32),
        scratch_shapes=[
            pltpu.VMEM((ROW_TILE, d), F32),
            pltpu.VMEM((ROW_TILE, d), BF16),
            pltpu.VMEM((ROW_TILE, FFN_DIM), BF16),
            pltpu.VMEM((SUBLANES, FFN_DIM), F32),
        ],
        compiler_params=_params("arbitrary"),
        name="layer_tail",
    )(*args)


def kernel(x, p, norm_mix, norm_ffn, norm_pl, norm_final, sb_w_in, sb_w_out, ret_w_in, ret_w_out, sgu_w_in, sgu_norm, sgu_w_s, sgu_b_s, sgu_w_out, ffn_w_in, ffn_conv_w, ffn_conv_b, ffn_w_out, pl_w_gate, pl_w_proj):
    b, s, d = x.shape
    rows = b * s
    xf = x.reshape(rows, d)
    pf = p.reshape(DEPTH, rows, PL_DIM)
    ret_tables = _retention_tables(s)
    for i in range(DEPTH):
        kind = i % N_MIXERS
        j = i // N_MIXERS
        if kind == 0:
            qkv = _norm_matmul(xf, norm_mix[i], sb_w_in[j].astype(BF16))
            mix = _sb_attention(qkv.reshape(b, s, 3 * d)).reshape(rows, d)
            w_mix = sb_w_out[j].astype(BF16)
        elif kind == 1:
            proj = _norm_matmul(xf, norm_mix[i], ret_w_in[j].astype(BF16))
            mix = _retention(proj.reshape(b, s, 6 * d), ret_tables).reshape(rows, 2 * d)
            w_mix = ret_w_out[j].astype(BF16)
        else:
            z = _norm_matmul(xf, norm_mix[i], sgu_w_in[j].astype(BF16), gelu=True)
            xf = _sgu(z, xf, sgu_norm[j], sgu_w_s[j], sgu_b_s[j], sgu_w_out[j].astype(BF16))
            mix = w_mix = None
        xf = _layer_tail(xf, mix, w_mix, norm_ffn[i], ffn_w_in[i].astype(BF16), ffn_conv_w[i], ffn_conv_b[i],
                         ffn_w_out[i].astype(BF16), pf, i, norm_pl[i], pl_w_gate[i].astype(BF16),
                         pl_w_proj[i].astype(BF16), norm_final, seq=s, final=(i == DEPTH - 1))
    return xf.reshape(b, s, d)
```
